```python
import math
import jax, jax.numpy as jnp
from jax import lax
import numpy as np

D_MODEL = 1024
BATCH = 16
SEQ = 256
DEPTH = 4
DEC_BATCH = 2
DEC_SEQ = 1024
PAST_LEN = 256

GRID_W = 64
DN_HEADS = D_MODEL // 128
DN_DK = 128
DN_DV = 128
DN_CHUNK = 64
CONV_K = 3
SC_WIDTH = D_MODEL
ATT_HEADS = D_MODEL // 128
ATT_KV_HEADS = ATT_HEADS // 4
HEAD_DIM = 128
Q_BLOCK = 128
ROPE_THETA = 10000.0
MLP_HIDDEN = 4 * D_MODEL
N_BRANCH = 3
RMS_EPS = 1e-6
IN_SIZES = (DN_HEADS * DN_DK, DN_HEADS * DN_DK, DN_HEADS * DN_DV, DN_HEADS * DN_DV, 2 * DN_HEADS, 2 * DN_HEADS, SC_WIDTH, SC_WIDTH, SC_WIDTH, ATT_HEADS * HEAD_DIM, ATT_KV_HEADS * HEAD_DIM, ATT_KV_HEADS * HEAD_DIM, N_BRANCH * D_MODEL)
IN_TOTAL = sum(IN_SIZES)

kernel_name = "hybrid_flow_prefix_trunk_step"


def split_cols(t, sizes):
    out, start = [], 0
    for s in sizes:
        out.append(t[..., start:start + s])
        start += s
    return out


def rmsnorm(x, g):
    xf = x.astype(jnp.float32)
    y = xf * lax.rsqrt(jnp.mean(xf * xf, axis=-1, keepdims=True) + RMS_EPS)
    return (y * g.astype(jnp.float32)).astype(x.dtype)


def l2norm(x):
    xf = x.astype(jnp.float32)
    return (xf * lax.rsqrt(jnp.sum(xf * xf, axis=-1, keepdims=True) + RMS_EPS)).astype(x.dtype)


def dwconv_centred(x, w):
    K, C = w.shape
    return lax.conv_general_dilated(x, w.reshape(K, 1, C).astype(x.dtype), window_strides=(1,), padding=((K // 2, K // 2),), dimension_numbers=("NWC", "WIO", "NWC"), feature_group_count=C)


def gated_delta_rule(q, k, v, g, beta, s0):
    B, L, H, DK = q.shape
    DV = v.shape[-1]
    C = DN_CHUNK
    N = L // C
    f32 = jnp.float32

    def chunks(t):
        t = t.astype(f32).reshape((B, N, C, H) + t.shape[3:])
        return jnp.moveaxis(t, (1, 3), (0, 2))

    qc = chunks(q) * (DK ** -0.5)
    kc = chunks(k)
    vc = chunks(v)
    bc = chunks(beta)
    gc = jnp.cumsum(chunks(g), axis=-1)
    incl = jnp.tril(jnp.ones((C, C), dtype=bool))
    strict = jnp.tril(jnp.ones((C, C), dtype=bool), -1)
    diff = gc[..., :, None] - gc[..., None, :]
    decay = jnp.where(incl, jnp.exp(jnp.where(incl, diff, 0.0)), 0.0)
    kb = kc * bc[..., None]
    m = jnp.where(strict, jnp.einsum("nbhid,nbhjd->nbhij", kb, kc) * decay, 0.0)
    a = m + jnp.eye(C, dtype=f32)
    u = lax.linalg.triangular_solve(a, vc * bc[..., None], left_side=True, lower=True)
    w = lax.linalg.triangular_solve(a, kb * jnp.exp(gc)[..., None], left_side=True, lower=True)
    qk = jnp.einsum("nbhid,nbhjd->nbhij", qc, kc) * decay
    q_dec = qc * jnp.exp(gc)[..., None]
    k_dec = kc * jnp.exp(gc[..., -1:] - gc)[..., None]
    g_last = jnp.exp(gc[..., -1])[..., None, None]

    def step(s, xs):
        u_n, w_n, qk_n, qd_n, kd_n, gl_n = xs
        v_new = u_n - jnp.einsum("bhcd,bhdv->bhcv", w_n, s)
        o_n = jnp.einsum("bhcd,bhdv->bhcv", qd_n, s) + jnp.einsum("bhij,bhjv->bhiv", qk_n, v_new)
        s = s * gl_n + jnp.einsum("bhcd,bhcv->bhdv", kd_n, v_new)
        return s, o_n

    s_fin, o = lax.scan(step, s0.astype(f32), (u, w, qk, q_dec, k_dec, g_last))
    o = jnp.moveaxis(o, (0, 2), (1, 3)).reshape(B, L, H, DV)
    return o.astype(v.dtype), s_fin.astype(s0.dtype)


def axial_angles(n_tokens):
    rows = n_tokens // GRID_W
    row = jnp.repeat(jnp.arange(rows), GRID_W).astype(jnp.float32)
    col = jnp.tile(jnp.arange(GRID_W), rows).astype(jnp.float32)
    n_freq = HEAD_DIM // 4
    freqs = ROPE_THETA ** (-jnp.arange(n_freq, dtype=jnp.float32) / n_freq)
    return row[:, None] * freqs, col[:, None] * freqs


def rope_axis(x, ang):
    m = ang.shape[-1]
    cos = jnp.cos(ang)[None, :, None, :]
    sin = jnp.sin(ang)[None, :, None, :]
    x1 = x[..., :m].astype(jnp.float32)
    x2 = x[..., m:].astype(jnp.float32)
    return jnp.concatenate([x1 * cos - x2 * sin, x1 * sin + x2 * cos], axis=-1).astype(x.dtype)


def rope2d(x, ang_row, ang_col):
    h = x.shape[-1] // 2
    return jnp.concatenate([rope_axis(x[..., :h], ang_row), rope_axis(x[..., h:], ang_col)], axis=-1)


def block_attention(q, keys, vals):
    B, Lq, H, HD = q.shape
    KV = keys.shape[2]
    G = H // KV
    nb = Lq // Q_BLOCK
    qb = jnp.moveaxis(q.reshape(B, nb, Q_BLOCK, KV, G, HD), 1, 0)

    def one_block(qblk):
        s = jnp.einsum("bqkgd,bskd->bkgqs", qblk, keys).astype(jnp.float32) * (HD ** -0.5)
        p = jax.nn.softmax(s, axis=-1).astype(vals.dtype)
        return jnp.einsum("bkgqs,bskd->bqkgd", p, vals)

    o = lax.map(one_block, qb)
    return jnp.moveaxis(o, 0, 1).reshape(B, Lq, H * HD)


def trunk_layer(x, cond, lw, ctx=None, rope=None):
    B, L, _ = x.shape
    mod = (jnp.dot(jax.nn.silu(cond), lw["w_mod"]) + lw["b_mod"]).reshape(-1, 1, 6 * D_MODEL)
    sh1, sc1, gt1, sh2, sc2, gt2 = jnp.split(mod, 6, axis=-1)
    h = rmsnorm(x, lw["g_pre_mix"]) * (1.0 + sc1) + sh1
    (dq, dk, dv, dgate, dbeta, dalpha, sb, scg, sx, aq, ak, av, mg) = split_cols(jnp.dot(h, lw["w_in"]), IN_SIZES)

    qkv = jax.nn.silu(dwconv_centred(jnp.concatenate([dq, dk, dv], axis=-1), lw["w_conv_dn"]))
    q, k, v = split_cols(qkv, IN_SIZES[:3])
    q = l2norm(q.reshape(B, L, DN_HEADS, DN_DK))
    k = l2norm(k.reshape(B, L, DN_HEADS, DN_DK))
    v = v.reshape(B, L, DN_HEADS, DN_DV)
    beta = jax.nn.sigmoid(dbeta.reshape(B, L, 2, DN_HEADS))
    g = -jnp.exp(lw["a_log"]) * jax.nn.softplus(dalpha.reshape(B, L, 2, DN_HEADS) + lw["dt_bias"])
    if ctx is None:
        s0 = jnp.zeros((B, 2, DN_HEADS, DN_DK, DN_DV), x.dtype)
    else:
        s0 = ctx[2]
    o_f, s_f = gated_delta_rule(q, k, v, g[:, :, 0], beta[:, :, 0], s0[:, 0])
    rev = lambda t: jnp.flip(t, axis=1)
    o_b, s_b = gated_delta_rule(rev(q), rev(k), rev(v), rev(g[:, :, 1]), rev(beta[:, :, 1]), s0[:, 1])
    o_dn = rmsnorm(o_f + rev(o_b), lw["g_dn_out"]) * jax.nn.silu(dgate.reshape(B, L, DN_HEADS, DN_DV))
    y_a = jnp.dot(o_dn.reshape(B, L, DN_HEADS * DN_DV), lw["w_br_dn"])

    y_b = jnp.dot(sb * dwconv_centred(scg * sx, lw["w_conv_sc"]), lw["w_br_sc"])

    qa = rmsnorm(aq.reshape(B, L, ATT_HEADS, HEAD_DIM), lw["g_q"])
    ka = rmsnorm(ak.reshape(B, L, ATT_KV_HEADS, HEAD_DIM), lw["g_k"])
    va = av.reshape(B, L, ATT_KV_HEADS, HEAD_DIM)
    if ctx is None:
        o_att = block_attention(qa, ka, va)
    else:
        ang_row, ang_col = rope
        keys = jnp.concatenate([ctx[0], rope2d(ka, ang_row, ang_col)], axis=1)
        vals = jnp.concatenate([ctx[1], va], axis=1)
        o_att = block_attention(rope2d(qa, ang_row, ang_col), keys, vals)
    y_c = jnp.dot(o_att, lw["w_br_att"])

    g_a, g_b, g_c = jnp.split(jax.nn.sigmoid(mg), 3, axis=-1)
    mix = jnp.dot(g_a * y_a + g_b * y_b + g_c * y_c, lw["w_out"])
    x = x + gt1 * rmsnorm(mix, lw["g_post_mix"])

    h2 = rmsnorm(x, lw["g_pre_mlp"]) * (1.0 + sc2) + sh2
    u = jax.nn.relu(jnp.dot(h2, lw["w_mlp_up"]))
    x = x + gt2 * rmsnorm(jnp.dot(u * u, lw["w_mlp_down"]), lw["g_post_mlp"])
    if ctx is None:
        return x, (ka, va, jnp.stack([s_f, s_b], axis=1))
    return x


def setup_inputs(seed: int = 0) -> dict:
    key = jax.random.key(seed)
    ks = jax.random.split(key, 32)
    f32 = jnp.float32

    def nrm(k, shape, scale):
        return scale * jax.random.normal(k, shape, f32)

    def gain(k, shape):
        return 1.0 + 0.05 * jax.random.normal(k, shape, f32)

    dt = jnp.exp(jax.random.uniform(ks[16], (DEPTH, 2, DN_HEADS), f32, math.log(1e-3), math.log(1e-1)))
    dt_bias = dt + jnp.log(-jnp.expm1(-dt))
    a_log = jnp.log(jax.random.uniform(ks[17], (DEPTH, 2, DN_HEADS), f32, 1.0, 16.0))
    return {
        "x_prompt": nrm(ks[0], (BATCH, SEQ, D_MODEL), 1.0),
        "x_sample": nrm(ks[1], (DEC_BATCH, DEC_SEQ, D_MODEL), 1.0),
        "cache_k": nrm(ks[2], (DEC_BATCH, DEPTH, PAST_LEN, ATT_KV_HEADS, HEAD_DIM), 1.0),
        "cache_v": nrm(ks[3], (DEC_BATCH, DEPTH, PAST_LEN, ATT_KV_HEADS, HEAD_DIM), 1.0),
        "state_dn": nrm(ks[4], (DEC_BATCH, DEPTH, 2, DN_HEADS, DN_DK, DN_DV), 0.1),
        "c": nrm(ks[5], (DEC_BATCH, D_MODEL), 1.0),
        "c_ctx": nrm(ks[6], (D_MODEL,), 1.0),
        "w_mod": nrm(ks[7], (DEPTH, D_MODEL, 6 * D_MODEL), 0.5 * D_MODEL ** -0.5),
        "b_mod": nrm(ks[8], (DEPTH, 6 * D_MODEL), 0.02),
        "g_pre_mix": gain(ks[9], (DEPTH, D_MODEL)),
        "g_post_mix": gain(ks[10], (DEPTH, D_MODEL)),
        "g_pre_mlp": gain(ks[11], (DEPTH, D_MODEL)),
        "g_post_mlp": gain(ks[12], (DEPTH, D_MODEL)),
        "w_in": nrm(ks[13], (DEPTH, D_MODEL, IN_TOTAL), D_MODEL ** -0.5),
        "w_conv_dn": nrm(ks[14], (DEPTH, CONV_K, 2 * DN_HEADS * DN_DK + DN_HEADS * DN_DV), CONV_K ** -0.5),
        "a_log": a_log,
        "dt_bias": dt_bias,
        "g_dn_out": gain(ks[15], (DEPTH, DN_DV)),
        "w_conv_sc": nrm(ks[18], (DEPTH, CONV_K, SC_WIDTH), CONV_K ** -0.5),
        "g_q": gain(ks[19], (DEPTH, HEAD_DIM)),
        "g_k": gain(ks[20], (DEPTH, HEAD_DIM)),
        "w_br_dn": nrm(ks[21], (DEPTH, DN_HEADS * DN_DV, D_MODEL), (DN_HEADS * DN_DV) ** -0.5),
        "w_br_sc": nrm(ks[22], (DEPTH, SC_WIDTH, D_MODEL), SC_WIDTH ** -0.5),
        "w_br_att": nrm(ks[23], (DEPTH, ATT_HEADS * HEAD_DIM, D_MODEL), (ATT_HEADS * HEAD_DIM) ** -0.5),
        "w_out": nrm(ks[24], (DEPTH, D_MODEL, D_MODEL), D_MODEL ** -0.5),
        "w_mlp_up": nrm(ks[25], (DEPTH, D_MODEL, MLP_HIDDEN), D_MODEL ** -0.5),
        "w_mlp_down": nrm(ks[26], (DEPTH, MLP_HIDDEN, D_MODEL), MLP_HIDDEN ** -0.5),
    }


def reference(x_prompt, x_sample, cache_k, cache_v, state_dn, c, c_ctx, w_mod, b_mod, g_pre_mix, g_post_mix, g_pre_mlp, g_post_mlp, w_in, w_conv_dn, a_log, dt_bias, g_dn_out, w_conv_sc, g_q, g_k, w_br_dn, w_br_sc, w_br_att, w_out, w_mlp_up, w_mlp_down):
    def layer_weights(l):
        return {"w_mod": w_mod[l], "b_mod": b_mod[l], "g_pre_mix": g_pre_mix[l], "g_post_mix": g_post_mix[l], "g_pre_mlp": g_pre_mlp[l], "g_post_mlp": g_post_mlp[l], "w_in": w_in[l], "w_conv_dn": w_conv_dn[l], "a_log": a_log[l], "dt_bias": dt_bias[l], "g_dn_out": g_dn_out[l], "w_conv_sc": w_conv_sc[l], "g_q": g_q[l], "g_k": g_k[l], "w_br_dn": w_br_dn[l], "w_br_sc": w_br_sc[l], "w_br_att": w_br_att[l], "w_out": w_out[l], "w_mlp_up": w_mlp_up[l], "w_mlp_down": w_mlp_down[l]}

    y_prompt = x_prompt
    ks, vs, ss = [], [], []
    for l in range(DEPTH):
        y_prompt, (k_l, v_l, s_l) = trunk_layer(y_prompt, c_ctx, layer_weights(l))
        ks.append(k_l)
        vs.append(v_l)
        ss.append(s_l)
    new_cache_k = jnp.stack(ks, axis=1)
    new_cache_v = jnp.stack(vs, axis=1)
    new_state_dn = jnp.stack(ss, axis=1)

    ang_row, ang_col = axial_angles(x_sample.shape[1])
    y_sample = x_sample
    for l in range(DEPTH):
        y_sample = trunk_layer(y_sample, c, layer_weights(l), ctx=(cache_k[:, l], cache_v[:, l], state_dn[:, l]), rope=(ang_row, ang_col))

    return (y_prompt, y_sample, new_cache_k, new_cache_v, new_state_dn)
```

```python
import functools

import jax
import jax.numpy as jnp
from jax import lax
from jax.experimental import pallas as pl
from jax.experimental.pallas import tpu as pltpu

D = 1024
BATCH, SEQ = 16, 256
DEC_BATCH, DEC_SEQ = 2, 1024
DEPTH = 4
PAST = 256
GRID_W = 64
NH = 8
HD = 128
KVH = 2
GQ = NH // KVH
CHUNK = 64
MLP_H = 4 * D
EPS = 1e-6
ROPE_THETA = 10000.0

T_CTX = BATCH * SEQ
T_SMP = DEC_BATCH * DEC_SEQ
T_ALL = T_CTX + T_SMP
TM = 512
N_CTX_TILES = T_CTX // TM
N_TILES = T_ALL // TM
TILES_PER_SMP = DEC_SEQ // TM

LANE = 128
CB_MG, CB_DQ, CB_DK, CB_DV, CB_DG, CB_SB, CB_SCG, CB_SX, CB_AQ, CB_AK, CB_AV, CB_GATE = 0, 24, 32, 40, 48, 56, 64, 72, 80, 88, 90, 92
N_CB = 93
PROJ_W = N_CB * LANE
PROJ_TN = 3 * LANE
ROW_CHUNK = 512

VMEM_LIMIT = 56 * 1024 * 1024

BF = jnp.bfloat16
F32 = jnp.float32
HIGHEST = lax.Precision.HIGHEST


def _params(*sem):
    return pltpu.CompilerParams(dimension_semantics=sem, vmem_limit_bytes=VMEM_LIMIT)


def _mm(a, b):
    return jnp.dot(a.astype(BF), b.astype(BF), preferred_element_type=F32)


def _mm_nt(a, b, precision=None):
    if precision is None:
        a, b = a.astype(BF), b.astype(BF)
    return lax.dot_general(a, b, (((1,), (1,)), ((), ())), precision=precision, preferred_element_type=F32)


def _mm_hi(a, b):
    return jnp.dot(a, b, precision=HIGHEST, preferred_element_type=F32)


def _rms(x, g):
    return x * lax.rsqrt(jnp.mean(x * x, axis=-1, keepdims=True) + EPS) * g


def _cond_row(i):
    return jnp.where(i < N_CTX_TILES, 0, 1 + (i - N_CTX_TILES) // TILES_PER_SMP)


def _mod_kernel(c_ref, w_ref, b_ref, o_ref):
    c = c_ref[...]
    s = c * jax.nn.sigmoid(c)
    o_ref[...] = _mm_hi(s, w_ref[...]) + b_ref[...]


def _mod_call(cond8, w_mod, b_mod):
    tn = 1536
    return pl.pallas_call(
        _mod_kernel,
        grid=(DEPTH, 6 * D // tn),
        in_specs=[
            pl.BlockSpec((8, D), lambda l, j: (0, 0)),
            pl.BlockSpec((None, D, tn), lambda l, j: (l, 0, j)),
            pl.BlockSpec((None, 1, tn), lambda l, j: (l, 0, j)),
        ],
        out_specs=pl.BlockSpec((None, 8, tn), lambda l, j: (l, 0, j)),
        out_shape=jax.ShapeDtypeStruct((DEPTH, 8, 6 * D), F32),
        compiler_params=_params("parallel", "parallel"),
        name="mod",
    )(cond8, w_mod, b_mod.reshape(DEPTH, 1, 6 * D))


def _mod_spec(l, chunk):
    return pl.BlockSpec((None, None, 1, D), lambda i: (l, _cond_row(i), 0, chunk))


def _gain_spec(l):
    return pl.BlockSpec((None, 1, D), lambda i: (l, 0, 0))


def _pre_kernel(x_ref, g_ref, sh_ref, sc_ref, h_ref):
    h = _rms(x_ref[...], g_ref[...]) * (1.0 + sc_ref[...]) + sh_ref[...]
    h_ref[...] = h.astype(BF)


def _pre_call(x, gain3, mod4, l, sh_chunk, sc_chunk):
    return pl.pallas_call(
        _pre_kernel,
        grid=(N_TILES,),
        in_specs=[
            pl.BlockSpec((TM, D), lambda i: (i, 0)),
            _gain_spec(l),
            _mod_spec(l, sh_chunk),
            _mod_spec(l, sc_chunk),
        ],
        out_specs=pl.BlockSpec((TM, D), lambda i: (i, 0)),
        out_shape=jax.ShapeDtypeStruct((T_ALL, D), BF),
        compiler_params=_params("parallel"),
        name="prenorm",
    )(x, gain3, mod4, mod4)


def _proj_kernel(h_ref, w_ref, o_ref):
    w = w_ref[...]

    def body(r, carry):
        r0 = pl.multiple_of(r * ROW_CHUNK, ROW_CHUNK)
        o_ref[pl.ds(r0, ROW_CHUNK), :] = jnp.dot(h_ref[pl.ds(r0, ROW_CHUNK), :], w, preferred_element_type=F32)
        return carry

    lax.fori_loop(0, T_ALL // ROW_CHUNK, body, 0)


def _proj_call(h, w_p, l):
    return pl.pallas_call(
        _proj_kernel,
        grid=(PROJ_W // PROJ_TN,),
        in_specs=[
            pl.BlockSpec((T_ALL, D), lambda j: (0, 0), pipeline_mode=pl.Buffered(1)),
            pl.BlockSpec((None, D, PROJ_TN), lambda j: (l, 0, j)),
        ],
        out_specs=pl.BlockSpec((T_ALL, PROJ_TN), lambda j: (0, j)),
        out_shape=jax.ShapeDtypeStruct((T_ALL, PROJ_W), F32),
        compiler_params=_params("parallel"),
        name="inproj",
    )(h, w_p)


def _dn_kernel(*refs, L, has_s0, emit_state):
    refs = list(refs)
    dq_ref, dk_ref, dv_ref, dg_ref, gt_ref, wq_ref, wk_ref, wv_ref, alog_ref, dtb_ref, gdn_ref = refs[:11]
    pos = 11
    s0_ref = None
    if has_s0:
        s0_ref = refs[pos]
        pos += 1
    o_ref = refs[pos]
    pos += 1
    sfin_ref = None
    if emit_state:
        sfin_ref = refs[pos]
        pos += 1
    q_s, k_s, v_s, gc_s, bt_s, o_s = refs[pos:]

    h = pl.program_id(1)
    n_chunks = L // CHUNK
    row = lax.broadcasted_iota(jnp.int32, (L, HD), 0)
    lane = lax.broadcasted_iota(jnp.int32, (L, HD), 1)
    cpos = row & (CHUNK - 1)

    def conv_silu(x_ref, w_ref):
        x = x_ref[...]
        w = w_ref[...]
        xm = jnp.where(row == 0, 0.0, pltpu.roll(x, 1, 0))
        xp = jnp.where(row == L - 1, 0.0, pltpu.roll(x, L - 1, 0))
        y = xm * w[0:1] + x * w[1:2] + xp * w[2:3]
        return y * jax.nn.sigmoid(y)

    def l2n(x):
        return x * lax.rsqrt(jnp.sum(x * x, axis=-1, keepdims=True) + EPS)

    q_s[...] = l2n(conv_silu(dq_ref, wq_ref)) * (HD ** -0.5)
    k_s[...] = l2n(conv_silu(dk_ref, wk_ref))
    v_s[...] = conv_silu(dv_ref, wv_ref)
    o_s[...] = jnp.zeros((L, HD), F32)

    gates = gt_ref[...]
    beta_all = jax.nn.sigmoid(gates)
    z = gates + dtb_ref[...]
    softplus = jnp.maximum(z, 0.0) + jnp.log(1.0 + jnp.exp(-jnp.abs(z)))
    g_all = -jnp.exp(alog_ref[...]) * softplus

    def column(arr, c):
        picked = jnp.sum(jnp.where(lane == c, arr, 0.0), axis=1, keepdims=True)
        return jnp.broadcast_to(picked, (L, HD))

    def chunk_cumsum(g, reverse):
        acc = g
        for s in (1, 2, 4, 8, 16, 32):
            if reverse:
                shifted = pltpu.roll(acc, L - s, 0)
                ok = cpos < CHUNK - s
            else:
                shifted = pltpu.roll(acc, s, 0)
                ok = cpos >= s
            acc = acc + jnp.where(ok, shifted, 0.0)
        return acc

    for d in range(2):
        bt_s[d] = column(beta_all, d * NH + h)
        gc_s[d] = chunk_cumsum(column(g_all, 2 * NH + d * NH + h), reverse=(d == 1))

    ii = lax.broadcasted_iota(jnp.int32, (CHUNK, CHUNK), 0)
    jj = lax.broadcasted_iota(jnp.int32, (CHUNK, CHUNK), 1)

    def chunk_step(c, s, d):
        incl = (ii >= jj) if d == 0 else (ii <= jj)
        strict = (ii > jj) if d == 0 else (ii < jj)
        r0 = pl.multiple_of(c * CHUNK, CHUNK)
        qc = q_s[pl.ds(r0, CHUNK), :]
        kc = k_s[pl.ds(r0, CHUNK), :]
        vc = v_s[pl.ds(r0, CHUNK), :]
        gcb = gc_s[d, pl.ds(r0, CHUNK), :]
        bb = bt_s[d, pl.ds(r0, CHUNK), :]
        kb = kc * bb
        gct = jnp.transpose(gcb)[:CHUNK, :]
        diff = gcb[:, :CHUNK] - gct
        decay = jnp.where(incl, jnp.exp(jnp.where(incl, diff, 0.0)), 0.0)
        a = jnp.where(strict, _mm_nt(kb, kc, HIGHEST) * decay, 0.0)
        eg = jnp.exp(gcb)
        x = jnp.concatenate([vc * bb, kb * eg], axis=1)
        x = x - _mm_hi(a, x)
        p = a
        for _ in range(5):
            p = _mm_hi(p, p)
            x = x + _mm_hi(p, x)
        u = x[:, :HD]
        w = x[:, HD:]
        qk = _mm_nt(qc, kc) * decay
        gend = gcb[CHUNK - 1:CHUNK, :] if d == 0 else gcb[0:1, :]
        kd = kc * jnp.exp(gend - gcb)
        v_new = u - _mm(w, s)
        o = _mm(qc * eg, s) + _mm(qk, v_new)
        s_next = s * jnp.exp(gend) + lax.dot_general(
            kd.astype(BF), v_new.astype(BF), (((0,), (0,)), ((), ())), preferred_element_type=F32)
        o_s[pl.ds(r0, CHUNK), :] += o
        return s_next

    def body(n, carry):
        s_f, s_b = carry
        return chunk_step(n, s_f, 0), chunk_step(n_chunks - 1 - n, s_b, 1)

    if has_s0:
        init = (s0_ref[0], s0_ref[1])
    else:
        init = (jnp.zeros((HD, HD), F32), jnp.zeros((HD, HD), F32))
    s_f, s_b = lax.fori_loop(0, n_chunks, body, init)
    if emit_state:
        sfin_ref[0] = s_f
        sfin_ref[1] = s_b

    dg = dg_ref[...]
    o_ref[...] = (_rms(o_s[...], gdn_ref[...]) * (dg * jax.nn.sigmoid(dg))).astype(BF)


def _dn_call(proj, w_conv, alog_l, dtb_l, gdn, state, l, *, n_batch, L, row_blk0, emit_state):
    has_s0 = state is not None

    def pspec(cb):
        return pl.BlockSpec((L, HD), lambda b, h: (row_blk0 + b, cb + h))

    def wspec(off):
        return pl.BlockSpec((None, 3, HD), lambda b, h: (l, 0, off + h))

    vec = pl.BlockSpec((1, HD), lambda b, h: (0, 0))
    in_specs = [pspec(CB_DQ), pspec(CB_DK), pspec(CB_DV), pspec(CB_DG),
                pl.BlockSpec((L, HD), lambda b, h: (row_blk0 + b, CB_GATE)),
                wspec(0), wspec(NH), wspec(2 * NH), vec, vec, vec]
    args = [proj, proj, proj, proj, proj, w_conv, w_conv, w_conv, alog_l, dtb_l, gdn]
    if has_s0:
        in_specs.append(pl.BlockSpec((None, None, 2, None, HD, HD), lambda b, h: (b, l, 0, h, 0, 0)))
        args.append(state)
    out_specs = [pl.BlockSpec((L, HD), lambda b, h: (b, h))]
    out_shape = [jax.ShapeDtypeStruct((n_batch * L, D), BF)]
    if emit_state:
        out_specs.append(pl.BlockSpec((None, 2, None, HD, HD), lambda b, h: (b, 0, h, 0, 0)))
        out_shape.append(jax.ShapeDtypeStruct((n_batch, 2, NH, HD, HD), F32))
    return pl.pallas_call(
        functools.partial(_dn_kernel, L=L, has_s0=has_s0, emit_state=emit_state),
        grid=(n_batch, NH),
        in_specs=in_specs,
        out_specs=out_specs,
        out_shape=out_shape,
        scratch_shapes=[
            pltpu.VMEM((L, HD), F32), pltpu.VMEM((L, HD), F32), pltpu.VMEM((L, HD), F32),
            pltpu.VMEM((2, L, HD), F32), pltpu.VMEM((2, L, HD), F32), pltpu.VMEM((L, HD), F32),
        ],
        compiler_params=_params("parallel", "parallel"),
        name="deltanet_ctx" if not has_s0 else "deltanet_smp",
    )(*args)


def _sc_kernel(sb_ref, scg_ref, sx_ref, w_ref, o_ref, *, L):
    tn = o_ref.shape[1]
    row = lax.broadcasted_iota(jnp.int32, (L, tn), 0)
    x = scg_ref[...] * sx_ref[...]
    w = w_ref[...]
    xm = jnp.where(row == 0, 0.0, pltpu.roll(x, 1, 0))
    xp = jnp.where(row == L - 1, 0.0, pltpu.roll(x, L - 1, 0))
    y = xm * w[0:1] + x * w[1:2] + xp * w[2:3]
    o_ref[...] = (sb_ref[...] * y).astype(BF)


def _sc_call(proj, w_conv_sc, l, *, n_batch, L, row_blk0):
    tn = 256
    per = LANE * 8 // tn

    def pspec(cb):
        return pl.BlockSpec((L, tn), lambda b, j: (row_blk0 + b, cb * LANE // tn + j))

    return pl.pallas_call(
        functools.partial(_sc_kernel, L=L),
        grid=(n_batch, D // tn),
        in_specs=[pspec(CB_SB), pspec(CB_SCG), pspec(CB_SX),
                  pl.BlockSpec((None, 3, tn), lambda b, j: (l, 0, j))],
        out_specs=pl.BlockSpec((L, tn), lambda b, j: (b, j)),
        out_shape=jax.ShapeDtypeStruct((n_batch * L, D), BF),
        compiler_params=_params("parallel", "parallel"),
        name="shortconv",
    )(proj, proj, proj, w_conv_sc)


def _rope(x, cos, sin_signed):
    lane = lax.broadcasted_iota(jnp.int32, x.shape, 1)
    swapped = jnp.where((lane & 63) < 32, pltpu.roll(x, HD - 32, 1), pltpu.roll(x, 32, 1))
    return x * cos + swapped * sin_signed


def _softmax_pv(s_parts, v_parts):
    m = s_parts[0].max(axis=-1, keepdims=True)
    for s in s_parts[1:]:
        m = jnp.maximum(m, s.max(axis=-1, keepdims=True))
    den = None
    acc = None
    for s, v in zip(s_parts, v_parts):
        p = jnp.exp(s - m)
        ps = jnp.sum(p, axis=-1, keepdims=True)
        pv = _mm(p, v)
        den = ps if den is None else den + ps
        acc = pv if acc is None else acc + pv
    return acc / den


def _att_ctx_kernel(q_ref, k_ref, v_ref, gq_ref, gk_ref, o_ref, ko_ref, vo_ref):
    k = _rms(k_ref[...], gk_ref[...])
    v = v_ref[...]
    ko_ref[...] = k
    vo_ref[...] = v
    gq = gq_ref[...] * (HD ** -0.5)
    q = jnp.concatenate([_rms(q_ref[:, g * HD:(g + 1) * HD], gq) for g in range(GQ)], axis=0)
    o = _softmax_pv([_mm_nt(q, k)], [v])
    for g in range(GQ):
        o_ref[:, g * HD:(g + 1) * HD] = o[g * SEQ:(g + 1) * SEQ].astype(BF)


def _att_ctx_call(proj, g_q, g_k):
    vec = pl.BlockSpec((1, HD), lambda b, g: (0, 0))
    return pl.pallas_call(
        _att_ctx_kernel,
        grid=(BATCH, KVH),
        in_specs=[
            pl.BlockSpec((SEQ, GQ * HD), lambda b, g: (b, CB_AQ // GQ + g)),
            pl.BlockSpec((SEQ, HD), lambda b, g: (b, CB_AK + g)),
            pl.BlockSpec((SEQ, HD), lambda b, g: (b, CB_AV + g)),
            vec, vec,
        ],
        out_specs=[
            pl.BlockSpec((SEQ, GQ * HD), lambda b, g: (b, g)),
            pl.BlockSpec((None, SEQ, HD), lambda b, g: (b, 0, g)),
            pl.BlockSpec((None, SEQ, HD), lambda b, g: (b, 0, g)),
        ],
        out_shape=[
            jax.ShapeDtypeStruct((T_CTX, D), BF),
            jax.ShapeDtypeStruct((BATCH, SEQ, KVH * HD), F32),
            jax.ShapeDtypeStruct((BATCH, SEQ, KVH * HD), F32),
        ],
        compiler_params=_params("parallel", "parallel"),
        name="attention_ctx",
    )(proj, proj, proj, g_q, g_k)


QB = 256


def _att_smp_kernel(q_ref, k_ref, v_ref, ck_ref, cv_ref, gq_ref, gk_ref, cosq_ref, sinq_ref, cosk_ref, sink_ref, o_ref):
    k = _rope(_rms(k_ref[...], gk_ref[...]), cosk_ref[...], sink_ref[...])
    gq = gq_ref[...] * (HD ** -0.5)
    cq, sq = cosq_ref[...], sinq_ref[...]
    q = jnp.concatenate([_rope(_rms(q_ref[:, g * HD:(g + 1) * HD], gq), cq, sq) for g in range(GQ)], axis=0)
    o = _softmax_pv([_mm_nt(q, ck_ref[...]), _mm_nt(q, k)], [cv_ref[...], v_ref[...]])
    for g in range(GQ):
        o_ref[:, g * HD:(g + 1) * HD] = o[g * QB:(g + 1) * QB].astype(BF)


def _att_smp_call(proj, cache_k4, cache_v4, g_q, g_k, cos_t, sin_t, l):
    nq = DEC_SEQ // QB
    smp_blk0 = T_CTX // DEC_SEQ
    q_blk0 = T_CTX // QB
    vec = pl.BlockSpec((1, HD), lambda b, g, i: (0, 0))
    tq = pl.BlockSpec((QB, HD), lambda b, g, i: (i, 0))
    tk = pl.BlockSpec((DEC_SEQ, HD), lambda b, g, i: (0, 0))
    cache = pl.BlockSpec((None, None, PAST, HD), lambda b, g, i: (b, l, 0, g))
    return pl.pallas_call(
        _att_smp_kernel,
        grid=(DEC_BATCH, KVH, nq),
        in_specs=[
            pl.BlockSpec((QB, GQ * HD), lambda b, g, i: (q_blk0 + b * nq + i, CB_AQ // GQ + g)),
            pl.BlockSpec((DEC_SEQ, HD), lambda b, g, i: (smp_blk0 + b, CB_AK + g)),
            pl.BlockSpec((DEC_SEQ, HD), lambda b, g, i: (smp_blk0 + b, CB_AV + g)),
            cache, cache, vec, vec, tq, tq, tk, tk,
        ],
        out_specs=pl.BlockSpec((QB, GQ * HD), lambda b, g, i: (b * nq + i, g)),
        out_shape=jax.ShapeDtypeStruct((T_SMP, D), BF),
        compiler_params=_params("parallel", "parallel", "parallel"),
        name="attention_smp",
    )(proj, proj, proj, cache_k4, cache_v4, g_q, g_k, cos_t, sin_t, cos_t, sin_t)


def _pick(i, ctx_ref, smp_ref):
    return jnp.where(i < N_CTX_TILES, ctx_ref[...], smp_ref[...])


def _mix_kernel(x_ref, ga_ref, gb_ref, gc_ref, ac_ref, as_ref, bc_ref, bs_ref, cc_ref, cs_ref,
                wa_ref, wb_ref, wc_ref, wo_ref, gpost_ref, gt_ref, o_ref):
    i = pl.program_id(0)
    m = jax.nn.sigmoid(ga_ref[...]) * jnp.dot(_pick(i, ac_ref, as_ref), wa_ref[...], preferred_element_type=F32)
    m += jax.nn.sigmoid(gb_ref[...]) * jnp.dot(_pick(i, bc_ref, bs_ref), wb_ref[...], preferred_element_type=F32)
    m += jax.nn.sigmoid(gc_ref[...]) * jnp.dot(_pick(i, cc_ref, cs_ref), wc_ref[...], preferred_element_type=F32)
    mix = jnp.dot(m.astype(BF), wo_ref[...], preferred_element_type=F32)
    o_ref[...] = x_ref[...] + gt_ref[...] * _rms(mix, gpost_ref[...])


def _resident(shape, index_map):
    return pl.BlockSpec(shape, index_map, pipeline_mode=pl.Buffered(1))


def _mix_call(x, proj, branches, weights, gain3, mod4, l):
    ctx_spec = pl.BlockSpec((TM, D), lambda i: (jnp.minimum(i, N_CTX_TILES - 1), 0))
    smp_spec = pl.BlockSpec((TM, D), lambda i: (jnp.maximum(i - N_CTX_TILES, 0), 0))
    wspec = _resident((None, D, D), lambda i: (l, 0, 0))

    def gate_spec(k):
        return pl.BlockSpec((TM, D), lambda i: (i, CB_MG // NH + k))

    in_specs = [pl.BlockSpec((TM, D), lambda i: (i, 0)), gate_spec(0), gate_spec(1), gate_spec(2)]
    args = [x, proj, proj, proj]
    for ctx, smp in branches:
        in_specs += [ctx_spec, smp_spec]
        args += [ctx, smp]
    in_specs += [wspec] * 4 + [_gain_spec(l), _mod_spec(l, 2)]
    args += list(weights) + [gain3, mod4]
    return pl.pallas_call(
        _mix_kernel,
        grid=(N_TILES,),
        in_specs=in_specs,
        out_specs=pl.BlockSpec((TM, D), lambda i: (i, 0)),
        out_shape=jax.ShapeDtypeStruct((T_ALL, D), F32),
        compiler_params=_params("parallel"),
        name="merge_outproj",
    )(*args)


def _mlp_kernel(x_ref, gpre_ref, sh_ref, sc_ref, wu_ref, wd_ref, gpost_ref, gt_ref, o_ref):
    x = x_ref[...]
    h = _rms(x, gpre_ref[...]) * (1.0 + sc_ref[...]) + sh_ref[...]
    u = jnp.maximum(jnp.dot(h.astype(BF), wu_ref[...], preferred_element_type=F32), 0.0)
    dn = jnp.dot((u * u).astype(BF), wd_ref[...], preferred_element_type=F32)
    o_ref[...] = x + gt_ref[...] * _rms(dn, gpost_ref[...])


def _mlp_call(x, w_up, w_down, gpre3, gpost3, mod4, l):
    return pl.pallas_call(
        _mlp_kernel,
        grid=(N_TILES,),
        in_specs=[
            pl.BlockSpec((TM, D), lambda i: (i, 0)),
            _gain_spec(l), _mod_spec(l, 3), _mod_spec(l, 4),
            _resident((None, D, MLP_H), lambda i: (l, 0, 0)),
            _resident((None, MLP_H, D), lambda i: (l, 0, 0)),
            _gain_spec(l), _mod_spec(l, 5),
        ],
        out_specs=pl.BlockSpec((TM, D), lambda i: (i, 0)),
        out_shape=jax.ShapeDtypeStruct((T_ALL, D), F32),
        compiler_params=_params("parallel"),
        name="mlp",
    )(x, gpre3, mod4, mod4, w_up, w_down, gpost3, mod4)


def _rope_tables():
    t = jnp.arange(DEC_SEQ)
    n_freq = HD // 4
    freqs = ROPE_THETA ** (-jnp.arange(n_freq, dtype=F32) / n_freq)
    ang_r = (t // GRID_W).astype(F32)[:, None] * freqs
    ang_c = (t % GRID_W).astype(F32)[:, None] * freqs
    cos_t = jnp.concatenate([jnp.cos(ang_r)] * 2 + [jnp.cos(ang_c)] * 2, axis=1)
    sin_t = jnp.concatenate([-jnp.sin(ang_r), jnp.sin(ang_r), -jnp.sin(ang_c), jnp.sin(ang_c)], axis=1)
    return cos_t, sin_t


def _gate_lanes(p):
    flat = p.reshape(DEPTH, 1, 2 * NH)
    return jnp.pad(flat, ((0, 0), (0, 0), (2 * NH, LANE - 4 * NH)))


def kernel(x_prompt, x_sample, cache_k, cache_v, state_dn, c, c_ctx, w_mod, b_mod, g_pre_mix, g_post_mix, g_pre_mlp, g_post_mlp, w_in, w_conv_dn, a_log, dt_bias, g_dn_out, w_conv_sc, g_q, g_k, w_br_dn, w_br_sc, w_br_att, w_out, w_mlp_up, w_mlp_down):
    x = jnp.concatenate([x_prompt.reshape(T_CTX, D), x_sample.reshape(T_SMP, D)], axis=0)

    cond8 = jnp.zeros((8, D), F32).at[0].set(c_ctx).at[1:1 + DEC_BATCH].set(c)
    mod4 = _mod_call(cond8, w_mod, b_mod).reshape(DEPTH, 8, 1, 6 * D)

    o = [0]
    for s in (NH * HD, NH * HD, NH * HD, NH * HD, 2 * NH, 2 * NH, D, D, D, NH * HD, KVH * HD, KVH * HD, 3 * D):
        o.append(o[-1] + s)
    seg = lambda k: w_in[:, :, o[k]:o[k + 1]]
    w_p = jnp.concatenate(
        [seg(12), seg(0), seg(1), seg(2), seg(3), seg(6), seg(7), seg(8), seg(9), seg(10), seg(11), seg(4), seg(5),
         jnp.zeros((DEPTH, D, LANE - 4 * NH), F32)], axis=2).astype(BF)

    w_a, w_b, w_c, w_o = (w.astype(BF) for w in (w_br_dn, w_br_sc, w_br_att, w_out))
    w_up, w_down = w_mlp_up.astype(BF), w_mlp_down.astype(BF)

    gpm3, gqm3 = g_pre_mix.reshape(DEPTH, 1, D), g_post_mix.reshape(DEPTH, 1, D)
    gpl3, gql3 = g_pre_mlp.reshape(DEPTH, 1, D), g_post_mlp.reshape(DEPTH, 1, D)
    alog_p, dtb_p = _gate_lanes(a_log), _gate_lanes(dt_bias)
    cos_t, sin_t = _rope_tables()
    cache_k4 = cache_k.reshape(DEC_BATCH, DEPTH, PAST, KVH * HD)
    cache_v4 = cache_v.reshape(DEC_BATCH, DEPTH, PAST, KVH * HD)

    ks, vs, ss = [], [], []
    for l in range(DEPTH):
        h = _pre_call(x, gpm3, mod4, l, 0, 1)
        proj = _proj_call(h, w_p, l)

        gdn = g_dn_out[l].reshape(1, HD)
        dn_ctx, s_ctx = _dn_call(proj, w_conv_dn, alog_p[l], dtb_p[l], gdn, None, l,
                                 n_batch=BATCH, L=SEQ, row_blk0=0, emit_state=True)
        (dn_smp,) = _dn_call(proj, w_conv_dn, alog_p[l], dtb_p[l], gdn, state_dn, l,
                             n_batch=DEC_BATCH, L=DEC_SEQ, row_blk0=T_CTX // DEC_SEQ, emit_state=False)
        sc_ctx = _sc_call(proj, w_conv_sc, l, n_batch=BATCH, L=SEQ, row_blk0=0)
        sc_smp = _sc_call(proj, w_conv_sc, l, n_batch=DEC_BATCH, L=DEC_SEQ, row_blk0=T_CTX // DEC_SEQ)
        gq, gk = g_q[l].reshape(1, HD), g_k[l].reshape(1, HD)
        at_ctx, k_l, v_l = _att_ctx_call(proj, gq, gk)
        at_smp = _att_smp_call(proj, cache_k4, cache_v4, gq, gk, cos_t, sin_t, l)

        x = _mix_call(x, proj, [(dn_ctx, dn_smp), (sc_ctx, sc_smp), (at_ctx, at_smp)],
                      (w_a, w_b, w_c, w_o), gqm3, mod4, l)
        x = _mlp_call(x, w_up, w_down, gpl3, gql3, mod4, l)
        ks.append(k_l)
        vs.append(v_l)
        ss.append(s_ctx)

    y_prompt = x[:T_CTX].reshape(BATCH, SEQ, D)
    y_sample = x[T_CTX:].reshape(DEC_BATCH, DEC_SEQ, D)
    new_k = jnp.stack(ks, axis=1).reshape(BATCH, DEPTH, SEQ, KVH, HD)
    new_v = jnp.stack(vs, axis=1).reshape(BATCH, DEPTH, SEQ, KVH, HD)
    new_s = jnp.stack(ss, axis=1)
    return (y_prompt, y_sample, new_k, new_v, new_s)
```

```python
import functools

import jax
import jax.numpy as jnp
from jax import lax
from jax.experimental import pallas as pl
from jax.experimental.pallas import tpu as pltpu

D = 1024
BATCH, SEQ = 16, 256
DEC_BATCH, DEC_SEQ = 2, 1024
DEPTH = 4
PAST = 256
GRID_W = 64
NH = 8
HD = 128
KVH = 2
GQ = NH // KVH
CHUNK = 64
MLP_H = 4 * D
EPS = 1e-6
ROPE_THETA = 10000.0

T_CTX = BATCH * SEQ
T_SMP = DEC_BATCH * DEC_SEQ
T_ALL = T_CTX + T_SMP
TM = 512
N_CTX_TILES = T_CTX // TM
N_TILES = T_ALL // TM
TILES_PER_SMP = DEC_SEQ // TM

LANE = 128
CB_MG, CB_DQ, CB_DK, CB_DV, CB_DG, CB_SB, CB_SCG, CB_SX, CB_AQ, CB_AK, CB_AV, CB_GATE = 0, 24, 32, 40, 48, 56, 64, 72, 80, 88, 90, 92
N_CB = 96
PROJ_W = N_CB * LANE
PROJ_TN = 4 * LANE
ROW_CHUNK = 512

VMEM_LIMIT = 56 * 1024 * 1024

BF = jnp.bfloat16
F32 = jnp.float32
HIGHEST = lax.Precision.HIGHEST


def _params(*sem):
    return pltpu.CompilerParams(dimension_semantics=sem, vmem_limit_bytes=VMEM_LIMIT)


def _mm(a, b):
    return jnp.dot(a.astype(BF), b.astype(BF), preferred_element_type=F32)


def _mm_nt(a, b, precision=None):
    if precision is None:
        a, b = a.astype(BF), b.astype(BF)
    return lax.dot_general(a, b, (((1,), (1,)), ((), ())), precision=precision, preferred_element_type=F32)


def _split(x):
    hi = x.astype(BF)
    return hi, (x - hi.astype(F32)).astype(BF)


def _mm_hi(a, b):
    return jnp.dot(a, b, precision=HIGHEST, preferred_element_type=F32)


def _rms(x, g):
    return x * lax.rsqrt(jnp.mean(x * x, axis=-1, keepdims=True) + EPS) * g


def _cond_row(i):
    return jnp.where(i < N_CTX_TILES, 0, 1 + (i - N_CTX_TILES) // TILES_PER_SMP)


def _mod_kernel(c_ref, w_ref, b_ref, o_ref):
    c = c_ref[...]
    s = c * jax.nn.sigmoid(c)
    o_ref[...] = _mm_hi(s, w_ref[...]) + b_ref[...]


def _mod_call(cond8, w_mod, b_mod):
    tn = 1536
    return pl.pallas_call(
        _mod_kernel,
        grid=(DEPTH, 6 * D // tn),
        in_specs=[
            pl.BlockSpec((8, D), lambda l, j: (0, 0)),
            pl.BlockSpec((None, D, tn), lambda l, j: (l, 0, j)),
            pl.BlockSpec((None, 1, tn), lambda l, j: (l, 0, j)),
        ],
        out_specs=pl.BlockSpec((None, 8, tn), lambda l, j: (l, 0, j)),
        out_shape=jax.ShapeDtypeStruct((DEPTH, 8, 6 * D), F32),
        compiler_params=_params("parallel", "parallel"),
        name="mod",
    )(cond8, w_mod, b_mod.reshape(DEPTH, 1, 6 * D))


def _mod_spec(l, chunk):
    return pl.BlockSpec((None, None, 1, D), lambda i: (l, _cond_row(i), 0, chunk))


def _gain_spec(l):
    return pl.BlockSpec((None, 1, D), lambda i: (l, 0, 0))


def _pre_kernel(x_ref, g_ref, sh_ref, sc_ref, h_ref):
    h = _rms(x_ref[...], g_ref[...]) * (1.0 + sc_ref[...]) + sh_ref[...]
    h_ref[...] = h.astype(BF)


def _pre_call(x, gain3, mod4, l, sh_chunk, sc_chunk):
    return pl.pallas_call(
        _pre_kernel,
        grid=(N_TILES,),
        in_specs=[
            pl.BlockSpec((TM, D), lambda i: (i, 0)),
            _gain_spec(l),
            _mod_spec(l, sh_chunk),
            _mod_spec(l, sc_chunk),
        ],
        out_specs=pl.BlockSpec((TM, D), lambda i: (i, 0)),
        out_shape=jax.ShapeDtypeStruct((T_ALL, D), BF),
        compiler_params=_params("parallel"),
        name="prenorm",
    )(x, gain3, mod4, mod4)


def _proj_kernel(h_ref, w_ref, o_ref):
    w = w_ref[...]

    def body(r, carry):
        r0 = pl.multiple_of(r * ROW_CHUNK, ROW_CHUNK)
        o_ref[pl.ds(r0, ROW_CHUNK), :] = jnp.dot(h_ref[pl.ds(r0, ROW_CHUNK), :], w, preferred_element_type=F32)
        return carry

    lax.fori_loop(0, T_ALL // ROW_CHUNK, body, 0)


def _proj_call(h, w_p, l):
    return pl.pallas_call(
        _proj_kernel,
        grid=(PROJ_W // PROJ_TN,),
        in_specs=[
            pl.BlockSpec((T_ALL, D), lambda j: (0, 0), pipeline_mode=pl.Buffered(1)),
            pl.BlockSpec((None, D, PROJ_TN), lambda j: (l, 0, j)),
        ],
        out_specs=pl.BlockSpec((T_ALL, PROJ_TN), lambda j: (0, j)),
        out_shape=jax.ShapeDtypeStruct((T_ALL, PROJ_W), F32),
        compiler_params=_params("parallel"),
        name="inproj",
    )(h, w_p)


HG = 4


def _dn_kernel(*refs, L, has_s0, emit_state):
    refs = list(refs)
    dq_ref, dk_ref, dv_ref, dg_ref, gt_ref, wq_ref, wk_ref, wv_ref, alog_ref, dtb_ref, gdn_ref = refs[:11]
    pos = 11
    s0_ref = None
    if has_s0:
        s0_ref = refs[pos]
        pos += 1
    o_ref = refs[pos]
    pos += 1
    sfin_ref = None
    if emit_state:
        sfin_ref = refs[pos]
        pos += 1
    a_s, b_s, c_s, gc_s, bt_s, u_s, wq_s, qkkd_s, ge_s, s_s = refs[pos:]

    head0 = pl.program_id(1) * HG
    n_chunks = L // CHUNK
    W = HG * HD
    row = lax.broadcasted_iota(jnp.int32, (L, W), 0)

    def conv_silu(x_ref, w_ref):
        x = x_ref[...]
        w = w_ref[...]
        xm = jnp.where(row == 0, 0.0, pltpu.roll(x, 1, 0))
        xp = jnp.where(row == L - 1, 0.0, pltpu.roll(x, L - 1, 0))
        y = xm * w[0:1] + x * w[1:2] + xp * w[2:3]
        return y * jax.nn.sigmoid(y)

    def l2n(x):
        return x * lax.rsqrt(jnp.sum(x * x, axis=-1, keepdims=True) + EPS)

    q = conv_silu(dq_ref, wq_ref)
    k = conv_silu(dk_ref, wk_ref)
    for h in range(HG):
        hs = slice(h * HD, (h + 1) * HD)
        a_s[:, hs] = l2n(q[:, hs]) * (HD ** -0.5)
        b_s[:, hs] = l2n(k[:, hs])
    c_s[...] = conv_silu(dv_ref, wv_ref)

    grow = lax.broadcasted_iota(jnp.int32, (L, LANE), 0)
    glane = lax.broadcasted_iota(jnp.int32, (L, LANE), 1)
    cpos = grow & (CHUNK - 1)
    gates = gt_ref[...]
    bt_s[...] = jax.nn.sigmoid(gates)
    z = gates + dtb_ref[...]
    softplus = jnp.maximum(z, 0.0) + jnp.log(1.0 + jnp.exp(-jnp.abs(z)))
    g_all = -jnp.exp(alog_ref[...]) * softplus
    prefix, suffix = g_all, g_all
    for s in (1, 2, 4, 8, 16, 32):
        prefix = prefix + jnp.where(cpos >= s, pltpu.roll(prefix, s, 0), 0.0)
        suffix = suffix + jnp.where(cpos < CHUNK - s, pltpu.roll(suffix, L - s, 0), 0.0)
    gc_s[...] = jnp.where(glane < 3 * NH, prefix, suffix)

    ii = lax.broadcasted_iota(jnp.int32, (CHUNK, 2 * CHUNK), 0)
    jj = lax.broadcasted_iota(jnp.int32, (CHUNK, 2 * CHUNK), 1) & (CHUNK - 1)
    clane = lax.broadcasted_iota(jnp.int32, (CHUNK, LANE), 1)

    def column(arr, c):
        picked = jnp.sum(jnp.where(clane == c, arr, 0.0), axis=1, keepdims=True)
        return jnp.broadcast_to(picked, (CHUNK, HD))

    chains = [(h, d) for h in range(HG) for d in range(2)]

    def pre_body(n, carry):
        r0 = pl.multiple_of(n * CHUNK, CHUNK)
        gcr = gc_s[pl.ds(r0, CHUNK), :]
        btr = bt_s[pl.ds(r0, CHUNK), :]
        qcs = [a_s[pl.ds(r0, CHUNK), h * HD:(h + 1) * HD] for h in range(HG)]
        kcs = [b_s[pl.ds(r0, CHUNK), h * HD:(h + 1) * HD] for h in range(HG)]
        raws = []
        for h in range(HG):
            lh, ll = _split(jnp.concatenate([kcs[h], qcs[h]], axis=0))
            rh, rl = _split(jnp.concatenate([kcs[h], kcs[h]], axis=0))
            raws.append(_mm_nt(jnp.concatenate([lh, lh, ll], axis=1), jnp.concatenate([rh, rl, rh], axis=1)))
        a_l, x_l = [], []
        for h, d in chains:
            incl = (ii >= jj) if d == 0 else (ii <= jj)
            strict = (ii > jj) if d == 0 else (ii < jj)
            col = d * NH + head0 + h
            bb = column(btr, col)
            gcb = column(gcr, 2 * NH + col)
            gct = jnp.transpose(jnp.concatenate([gcb, gcb], axis=0))[:CHUNK, :]
            decay = jnp.where(incl, jnp.exp(jnp.where(incl, gcb - gct, 0.0)), 0.0)
            eg = jnp.exp(gcb)
            gend = gcb[CHUNK - 1:CHUNK, :] if d == 0 else gcb[0:1, :]
            kc = kcs[h]
            vc = c_s[pl.ds(r0, CHUNK), h * HD:(h + 1) * HD]
            wq_s[d, h, n, CHUNK:, :] = (qcs[h] * eg).astype(BF)
            qkkd_s[d, h, n, 0:CHUNK, :] = (raws[h][CHUNK:, :CHUNK] * decay[:, :CHUNK]).astype(BF)
            qkkd_s[d, h, n, CHUNK:, :] = jnp.transpose(kc * jnp.exp(gend - gcb)).astype(BF)
            ge_s[d, h, n] = jnp.broadcast_to(jnp.exp(gend), (8, HD))
            a_l.append(jnp.where(strict, bb * raws[h][:CHUNK] * decay, 0.0))
            x_l.append(jnp.concatenate([vc * bb, kc * (bb * eg)], axis=1))
        p_l = a_l
        for level in range(6):
            ax_l, pp_l = [], []
            for p, x in zip(p_l, x_l):
                ph, pw = _split(p)
                xh, xw = _split(x)
                lhs = jnp.concatenate([ph, pw[:, :CHUNK]], axis=1)
                ax_l.append(jnp.dot(lhs, jnp.concatenate([xh, xw, xh], axis=0), preferred_element_type=F32))
                if level < 5:
                    pp_l.append(jnp.dot(lhs, jnp.concatenate([ph, pw, ph], axis=0), preferred_element_type=F32))
            p_l = pp_l
            x_l = [x - ax if level == 0 else x + ax for x, ax in zip(x_l, ax_l)]
        for (h, d), x in zip(chains, x_l):
            u_s[d, h, pl.ds(r0, CHUNK), :] = x[:, :HD]
            wq_s[d, h, n, 0:CHUNK, :] = x[:, HD:].astype(BF)
        return carry

    lax.fori_loop(0, n_chunks, pre_body, 0)

    for h, d in chains:
        s_s[d, h] = s0_ref[d, h] if has_s0 else jnp.zeros((HD, HD), F32)

    def scan_body(n, carry):
        cs = [n, n_chunks - 1 - n]
        r0s = [pl.multiple_of(c * CHUNK, CHUNK) for c in cs]
        s_l = [s_s[d, h] for h, d in chains]
        ws_l = [jnp.dot(wq_s[d, h, cs[d]], s.astype(BF), preferred_element_type=F32)
                for (h, d), s in zip(chains, s_l)]
        v_l = [u_s[d, h, pl.ds(r0s[d], CHUNK), :] - ws[:CHUNK] for (h, d), ws in zip(chains, ws_l)]
        kv_l = [jnp.dot(qkkd_s[d, h, cs[d]], v.astype(BF), preferred_element_type=F32)
                for (h, d), v in zip(chains, v_l)]
        for (h, d), s, ws, kv in zip(chains, s_l, ws_l, kv_l):
            s_s[d, h] = s * ge_s[d, h, cs[d]][0:1] + kv[CHUNK:]
            out_s = a_s if d == 0 else b_s
            out_s[pl.ds(r0s[d], CHUNK), h * HD:(h + 1) * HD] = ws[CHUNK:] + kv[:CHUNK]
        return carry

    lax.fori_loop(0, n_chunks, scan_body, 0)

    if emit_state:
        for d in range(2):
            for h in range(HG):
                sfin_ref[d, h] = s_s[d, h]

    gdn = gdn_ref[...]
    for h in range(HG):
        hs = slice(h * HD, (h + 1) * HD)
        dg = dg_ref[:, hs]
        o_ref[:, hs] = (_rms(a_s[:, hs] + b_s[:, hs], gdn) * (dg * jax.nn.sigmoid(dg))).astype(BF)


def _dn_call(proj, w_conv, alog_l, dtb_l, gdn, state, l, *, n_batch, L, row_blk0, emit_state):
    has_s0 = state is not None
    n_chunks = L // CHUNK
    W = HG * HD

    def pspec(cb):
        return pl.BlockSpec((L, W), lambda b, j: (row_blk0 + b, cb // HG + j))

    def wspec(off):
        return pl.BlockSpec((None, 3, W), lambda b, j: (l, 0, off // HG + j))

    vec = pl.BlockSpec((1, HD), lambda b, j: (0, 0))
    in_specs = [pspec(CB_DQ), pspec(CB_DK), pspec(CB_DV), pspec(CB_DG),
                pl.BlockSpec((L, LANE), lambda b, j: (row_blk0 + b, CB_GATE)),
                wspec(0), wspec(NH), wspec(2 * NH), vec, vec, vec]
    args = [proj, proj, proj, proj, proj, w_conv, w_conv, w_conv, alog_l, dtb_l, gdn]
    if has_s0:
        in_specs.append(pl.BlockSpec((None, None, 2, HG, HD, HD), lambda b, j: (b, l, 0, j, 0, 0)))
        args.append(state)
    out_specs = [pl.BlockSpec((L, W), lambda b, j: (b, j))]
    out_shape = [jax.ShapeDtypeStruct((n_batch * L, D), BF)]
    if emit_state:
        out_specs.append(pl.BlockSpec((None, 2, HG, HD, HD), lambda b, j: (b, 0, j, 0, 0)))
        out_shape.append(jax.ShapeDtypeStruct((n_batch, 2, NH, HD, HD), F32))
    return pl.pallas_call(
        functools.partial(_dn_kernel, L=L, has_s0=has_s0, emit_state=emit_state),
        grid=(n_batch, NH // HG),
        in_specs=in_specs,
        out_specs=out_specs,
        out_shape=out_shape,
        scratch_shapes=[
            pltpu.VMEM((L, W), F32), pltpu.VMEM((L, W), F32), pltpu.VMEM((L, W), F32),
            pltpu.VMEM((L, LANE), F32), pltpu.VMEM((L, LANE), F32),
            pltpu.VMEM((2, HG, L, HD), F32),
            pltpu.VMEM((2, HG, n_chunks, 2 * CHUNK, HD), BF),
            pltpu.VMEM((2, HG, n_chunks, CHUNK + HD, CHUNK), BF),
            pltpu.VMEM((2, HG, n_chunks, 8, HD), F32),
            pltpu.VMEM((2, HG, HD, HD), F32),
        ],
        compiler_params=_params("parallel", "parallel"),
        name="deltanet_ctx" if not has_s0 else "deltanet_smp",
    )(*args)


def _sc_kernel(sb_ref, scg_ref, sx_ref, w_ref, o_ref, *, L):
    tn = o_ref.shape[1]
    row = lax.broadcasted_iota(jnp.int32, (L, tn), 0)
    x = scg_ref[...] * sx_ref[...]
    w = w_ref[...]
    xm = jnp.where(row == 0, 0.0, pltpu.roll(x, 1, 0))
    xp = jnp.where(row == L - 1, 0.0, pltpu.roll(x, L - 1, 0))
    y = xm * w[0:1] + x * w[1:2] + xp * w[2:3]
    o_ref[...] = (sb_ref[...] * y).astype(BF)


def _sc_call(proj, w_conv_sc, l, *, n_batch, L, row_blk0):
    tn = 256

    def pspec(cb):
        return pl.BlockSpec((L, tn), lambda b, j: (row_blk0 + b, cb * LANE // tn + j))

    return pl.pallas_call(
        functools.partial(_sc_kernel, L=L),
        grid=(n_batch, D // tn),
        in_specs=[pspec(CB_SB), pspec(CB_SCG), pspec(CB_SX),
                  pl.BlockSpec((None, 3, tn), lambda b, j: (l, 0, j))],
        out_specs=pl.BlockSpec((L, tn), lambda b, j: (b, j)),
        out_shape=jax.ShapeDtypeStruct((n_batch * L, D), BF),
        compiler_params=_params("parallel", "parallel"),
        name="shortconv",
    )(proj, proj, proj, w_conv_sc)


def _rope(x, cos, sin_signed):
    lane = lax.broadcasted_iota(jnp.int32, x.shape, 1)
    swapped = jnp.where((lane & 63) < 32, pltpu.roll(x, HD - 32, 1), pltpu.roll(x, 32, 1))
    return x * cos + swapped * sin_signed


def _softmax_pv(s_parts, v_parts):
    m = s_parts[0].max(axis=-1, keepdims=True)
    for s in s_parts[1:]:
        m = jnp.maximum(m, s.max(axis=-1, keepdims=True))
    den = None
    acc = None
    for s, v in zip(s_parts, v_parts):
        p = jnp.exp(s - m)
        ps = jnp.sum(p, axis=-1, keepdims=True)
        pv = _mm(p, v)
        den = ps if den is None else den + ps
        acc = pv if acc is None else acc + pv
    return acc / den


def _att_ctx_kernel(q_ref, k_ref, v_ref, gq_ref, gk_ref, o_ref, ko_ref, vo_ref):
    k = _rms(k_ref[...], gk_ref[...])
    v = v_ref[...]
    ko_ref[...] = k
    vo_ref[...] = v
    gq = gq_ref[...] * (HD ** -0.5)
    q = jnp.concatenate([_rms(q_ref[:, g * HD:(g + 1) * HD], gq) for g in range(GQ)], axis=0)
    o = _softmax_pv([_mm_nt(q, k)], [v])
    for g in range(GQ):
        o_ref[:, g * HD:(g + 1) * HD] = o[g * SEQ:(g + 1) * SEQ].astype(BF)


def _att_ctx_call(proj, g_q, g_k):
    vec = pl.BlockSpec((1, HD), lambda b, g: (0, 0))
    return pl.pallas_call(
        _att_ctx_kernel,
        grid=(BATCH, KVH),
        in_specs=[
            pl.BlockSpec((SEQ, GQ * HD), lambda b, g: (b, CB_AQ // GQ + g)),
            pl.BlockSpec((SEQ, HD), lambda b, g: (b, CB_AK + g)),
            pl.BlockSpec((SEQ, HD), lambda b, g: (b, CB_AV + g)),
            vec, vec,
        ],
        out_specs=[
            pl.BlockSpec((SEQ, GQ * HD), lambda b, g: (b, g)),
            pl.BlockSpec((None, SEQ, HD), lambda b, g: (b, 0, g)),
            pl.BlockSpec((None, SEQ, HD), lambda b, g: (b, 0, g)),
        ],
        out_shape=[
            jax.ShapeDtypeStruct((T_CTX, D), BF),
            jax.ShapeDtypeStruct((BATCH, SEQ, KVH * HD), F32),
            jax.ShapeDtypeStruct((BATCH, SEQ, KVH * HD), F32),
        ],
        compiler_params=_params("parallel", "parallel"),
        name="attention_ctx",
    )(proj, proj, proj, g_q, g_k)


QB = 256


def _att_smp_kernel(q_ref, k_ref, v_ref, ck_ref, cv_ref, gq_ref, gk_ref, cosq_ref, sinq_ref, cosk_ref, sink_ref, o_ref):
    k = _rope(_rms(k_ref[...], gk_ref[...]), cosk_ref[...], sink_ref[...])
    gq = gq_ref[...] * (HD ** -0.5)
    cq, sq = cosq_ref[...], sinq_ref[...]
    q = jnp.concatenate([_rope(_rms(q_ref[:, g * HD:(g + 1) * HD], gq), cq, sq) for g in range(GQ)], axis=0)
    o = _softmax_pv([_mm_nt(q, ck_ref[...]), _mm_nt(q, k)], [cv_ref[...], v_ref[...]])
    for g in range(GQ):
        o_ref[:, g * HD:(g + 1) * HD] = o[g * QB:(g + 1) * QB].astype(BF)


def _att_smp_call(proj, cache_k4, cache_v4, g_q, g_k, cos_t, sin_t, l):
    nq = DEC_SEQ // QB
    smp_blk0 = T_CTX // DEC_SEQ
    q_blk0 = T_CTX // QB
    vec = pl.BlockSpec((1, HD), lambda b, g, i: (0, 0))
    tq = pl.BlockSpec((QB, HD), lambda b, g, i: (i, 0))
    tk = pl.BlockSpec((DEC_SEQ, HD), lambda b, g, i: (0, 0))
    cache = pl.BlockSpec((None, None, PAST, HD), lambda b, g, i: (b, l, 0, g))
    return pl.pallas_call(
        _att_smp_kernel,
        grid=(DEC_BATCH, KVH, nq),
        in_specs=[
            pl.BlockSpec((QB, GQ * HD), lambda b, g, i: (q_blk0 + b * nq + i, CB_AQ // GQ + g)),
            pl.BlockSpec((DEC_SEQ, HD), lambda b, g, i: (smp_blk0 + b, CB_AK + g)),
            pl.BlockSpec((DEC_SEQ, HD), lambda b, g, i: (smp_blk0 + b, CB_AV + g)),
            cache, cache, vec, vec, tq, tq, tk, tk,
        ],
        out_specs=pl.BlockSpec((QB, GQ * HD), lambda b, g, i: (b * nq + i, g)),
        out_shape=jax.ShapeDtypeStruct((T_SMP, D), BF),
        compiler_params=_params("parallel", "parallel", "parallel"),
        name="attention_smp",
    )(proj, proj, proj, cache_k4, cache_v4, g_q, g_k, cos_t, sin_t, cos_t, sin_t)


def _pick(i, ctx_ref, smp_ref):
    return jnp.where(i < N_CTX_TILES, ctx_ref[...], smp_ref[...])


def _mix_kernel(x_ref, ga_ref, gb_ref, gc_ref, ac_ref, as_ref, bc_ref, bs_ref, cc_ref, cs_ref,
                wa_ref, wb_ref, wc_ref, wo_ref, gpost_ref, gt_ref, o_ref):
    i = pl.program_id(0)
    m = jax.nn.sigmoid(ga_ref[...]) * jnp.dot(_pick(i, ac_ref, as_ref), wa_ref[...], preferred_element_type=F32)
    m += jax.nn.sigmoid(gb_ref[...]) * jnp.dot(_pick(i, bc_ref, bs_ref), wb_ref[...], preferred_element_type=F32)
    m += jax.nn.sigmoid(gc_ref[...]) * jnp.dot(_pick(i, cc_ref, cs_ref), wc_ref[...], preferred_element_type=F32)
    mix = jnp.dot(m.astype(BF), wo_ref[...], preferred_element_type=F32)
    o_ref[...] = x_ref[...] + gt_ref[...] * _rms(mix, gpost_ref[...])


def _resident(shape, index_map):
    return pl.BlockSpec(shape, index_map, pipeline_mode=pl.Buffered(1))


def _mix_call(x, proj, branches, weights, gain3, mod4, l):
    ctx_spec = pl.BlockSpec((TM, D), lambda i: (jnp.minimum(i, N_CTX_TILES - 1), 0))
    smp_spec = pl.BlockSpec((TM, D), lambda i: (jnp.maximum(i - N_CTX_TILES, 0), 0))
    wspec = _resident((None, D, D), lambda i: (l, 0, 0))

    def gate_spec(k):
        return pl.BlockSpec((TM, D), lambda i: (i, CB_MG // NH + k))

    in_specs = [pl.BlockSpec((TM, D), lambda i: (i, 0)), gate_spec(0), gate_spec(1), gate_spec(2)]
    args = [x, proj, proj, proj]
    for ctx, smp in branches:
        in_specs += [ctx_spec, smp_spec]
        args += [ctx, smp]
    in_specs += [wspec] * 4 + [_gain_spec(l), _mod_spec(l, 2)]
    args += list(weights) + [gain3, mod4]
    return pl.pallas_call(
        _mix_kernel,
        grid=(N_TILES,),
        in_specs=in_specs,
        out_specs=pl.BlockSpec((TM, D), lambda i: (i, 0)),
        out_shape=jax.ShapeDtypeStruct((T_ALL, D), F32),
        compiler_params=_params("parallel"),
        name="merge_outproj",
    )(*args)


def _mlp_kernel(x_ref, gpre_ref, sh_ref, sc_ref, wu_ref, wd_ref, gpost_ref, gt_ref, o_ref):
    x = x_ref[...]
    h = _rms(x, gpre_ref[...]) * (1.0 + sc_ref[...]) + sh_ref[...]
    u = jnp.maximum(jnp.dot(h.astype(BF), wu_ref[...], preferred_element_type=F32), 0.0)
    dn = jnp.dot((u * u).astype(BF), wd_ref[...], preferred_element_type=F32)
    o_ref[...] = x + gt_ref[...] * _rms(dn, gpost_ref[...])


def _mlp_call(x, w_up, w_down, gpre3, gpost3, mod4, l):
    return pl.pallas_call(
        _mlp_kernel,
        grid=(N_TILES,),
        in_specs=[
            pl.BlockSpec((TM, D), lambda i: (i, 0)),
            _gain_spec(l), _mod_spec(l, 3), _mod_spec(l, 4),
            _resident((None, D, MLP_H), lambda i: (l, 0, 0)),
            _resident((None, MLP_H, D), lambda i: (l, 0, 0)),
            _gain_spec(l), _mod_spec(l, 5),
        ],
        out_specs=pl.BlockSpec((TM, D), lambda i: (i, 0)),
        out_shape=jax.ShapeDtypeStruct((T_ALL, D), F32),
        compiler_params=_params("parallel"),
        name="mlp",
    )(x, gpre3, mod4, mod4, w_up, w_down, gpost3, mod4)


def _rope_tables():
    t = jnp.arange(DEC_SEQ)
    n_freq = HD // 4
    freqs = ROPE_THETA ** (-jnp.arange(n_freq, dtype=F32) / n_freq)
    ang_r = (t // GRID_W).astype(F32)[:, None] * freqs
    ang_c = (t % GRID_W).astype(F32)[:, None] * freqs
    cos_t = jnp.concatenate([jnp.cos(ang_r)] * 2 + [jnp.cos(ang_c)] * 2, axis=1)
    sin_t = jnp.concatenate([-jnp.sin(ang_r), jnp.sin(ang_r), -jnp.sin(ang_c), jnp.sin(ang_c)], axis=1)
    return cos_t, sin_t


def _gate_lanes(p):
    flat = p.reshape(DEPTH, 1, 2 * NH)
    return jnp.pad(flat, ((0, 0), (0, 0), (2 * NH, LANE - 4 * NH)))


def kernel(x_prompt, x_sample, cache_k, cache_v, state_dn, c, c_ctx, w_mod, b_mod, g_pre_mix, g_post_mix, g_pre_mlp, g_post_mlp, w_in, w_conv_dn, a_log, dt_bias, g_dn_out, w_conv_sc, g_q, g_k, w_br_dn, w_br_sc, w_br_att, w_out, w_mlp_up, w_mlp_down):
    x = jnp.concatenate([x_prompt.reshape(T_CTX, D), x_sample.reshape(T_SMP, D)], axis=0)

    cond8 = jnp.zeros((8, D), F32).at[0].set(c_ctx).at[1:1 + DEC_BATCH].set(c)
    mod4 = _mod_call(cond8, w_mod, b_mod).reshape(DEPTH, 8, 1, 6 * D)

    o = [0]
    for s in (NH * HD, NH * HD, NH * HD, NH * HD, 2 * NH, 2 * NH, D, D, D, NH * HD, KVH * HD, KVH * HD, 3 * D):
        o.append(o[-1] + s)
    seg = lambda k: w_in[:, :, o[k]:o[k + 1]]
    w_p = jnp.concatenate(
        [seg(12), seg(0), seg(1), seg(2), seg(3), seg(6), seg(7), seg(8), seg(9), seg(10), seg(11), seg(4), seg(5),
         jnp.zeros((DEPTH, D, PROJ_W - (CB_GATE * LANE + 4 * NH)), F32)], axis=2).astype(BF)

    w_a, w_b, w_c, w_o = (w.astype(BF) for w in (w_br_dn, w_br_sc, w_br_att, w_out))
    w_up, w_down = w_mlp_up.astype(BF), w_mlp_down.astype(BF)

    gpm3, gqm3 = g_pre_mix.reshape(DEPTH, 1, D), g_post_mix.reshape(DEPTH, 1, D)
    gpl3, gql3 = g_pre_mlp.reshape(DEPTH, 1, D), g_post_mlp.reshape(DEPTH, 1, D)
    alog_p, dtb_p = _gate_lanes(a_log), _gate_lanes(dt_bias)
    cos_t, sin_t = _rope_tables()
    cache_k4 = cache_k.reshape(DEC_BATCH, DEPTH, PAST, KVH * HD)
    cache_v4 = cache_v.reshape(DEC_BATCH, DEPTH, PAST, KVH * HD)

    ks, vs, ss = [], [], []
    for l in range(DEPTH):
        h = _pre_call(x, gpm3, mod4, l, 0, 1)
        proj = _proj_call(h, w_p, l)

        gdn = g_dn_out[l].reshape(1, HD)
        dn_ctx, s_ctx = _dn_call(proj, w_conv_dn, alog_p[l], dtb_p[l], gdn, None, l,
                                 n_batch=BATCH, L=SEQ, row_blk0=0, emit_state=True)
        (dn_smp,) = _dn_call(proj, w_conv_dn, alog_p[l], dtb_p[l], gdn, state_dn, l,
                             n_batch=DEC_BATCH, L=DEC_SEQ, row_blk0=T_CTX // DEC_SEQ, emit_state=False)
        sc_ctx = _sc_call(proj, w_conv_sc, l, n_batch=BATCH, L=SEQ, row_blk0=0)
        sc_smp = _sc_call(proj, w_conv_sc, l, n_batch=DEC_BATCH, L=DEC_SEQ, row_blk0=T_CTX // DEC_SEQ)
        gq, gk = g_q[l].reshape(1, HD), g_k[l].reshape(1, HD)
        at_ctx, k_l, v_l = _att_ctx_call(proj, gq, gk)
        at_smp = _att_smp_call(proj, cache_k4, cache_v4, gq, gk, cos_t, sin_t, l)

        x = _mix_call(x, proj, [(dn_ctx, dn_smp), (sc_ctx, sc_smp), (at_ctx, at_smp)],
                      (w_a, w_b, w_c, w_o), gqm3, mod4, l)
        x = _mlp_call(x, w_up, w_down, gpl3, gql3, mod4, l)
        ks.append(k_l)
        vs.append(v_l)
        ss.append(s_ctx)

    y_prompt = x[:T_CTX].reshape(BATCH, SEQ, D)
    y_sample = x[T_CTX:].reshape(DEC_BATCH, DEC_SEQ, D)
    new_k = jnp.stack(ks, axis=1).reshape(BATCH, DEPTH, SEQ, KVH, HD)
    new_v = jnp.stack(vs, axis=1).reshape(BATCH, DEPTH, SEQ, KVH, HD)
    new_s = jnp.stack(ss, axis=1)
    return (y_prompt, y_sample, new_k, new_v, new_s)
```

```python
import functools

import jax
import jax.numpy as jnp
from jax import lax
from jax.experimental import pallas as pl
from jax.experimental.pallas import tpu as pltpu

D = 1024
BATCH, SEQ = 16, 256
DEC_BATCH, DEC_SEQ = 2, 1024
DEPTH = 4
PAST = 256
GRID_W = 64
NH = 8
HD = 128
KVH = 2
GQ = NH // KVH
CHUNK = 64
MLP_H = 4 * D
EPS = 1e-6
ROPE_THETA = 10000.0

T_CTX = BATCH * SEQ
T_SMP = DEC_BATCH * DEC_SEQ
T_ALL = T_CTX + T_SMP
TM = 512
N_CTX_TILES = T_CTX // TM
N_TILES = T_ALL // TM
TILES_PER_SMP = DEC_SEQ // TM

LANE = 128
CB_DQ, CB_DK, CB_DV, CB_DG, CB_SB, CB_SCG, CB_SX, CB_AQ, CB_MG, CB_AK, CB_AV, CB_GATE = 0, 8, 16, 24, 32, 40, 48, 56, 64, 88, 90, 92
N_CB = 96
PROJ_W = N_CB * LANE
PROJ_TN = 4 * LANE
ROW_CHUNK = 2048

VMEM_LIMIT = 56 * 1024 * 1024

BF = jnp.bfloat16
F32 = jnp.float32
HIGHEST = lax.Precision.HIGHEST


def _params(*sem):
    return pltpu.CompilerParams(dimension_semantics=sem, vmem_limit_bytes=VMEM_LIMIT)


def _mm(a, b):
    return jnp.dot(a.astype(BF), b.astype(BF), preferred_element_type=F32)


def _mm_nt(a, b, precision=None):
    if precision is None:
        a, b = a.astype(BF), b.astype(BF)
    return lax.dot_general(a, b, (((1,), (1,)), ((), ())), precision=precision, preferred_element_type=F32)


def _split(x):
    hi = x.astype(BF)
    return hi, (x - hi.astype(F32)).astype(BF)


def _mm_hi(a, b):
    return jnp.dot(a, b, precision=HIGHEST, preferred_element_type=F32)


def _rms(x, g):
    return x * lax.rsqrt(jnp.mean(x * x, axis=-1, keepdims=True) + EPS) * g


def _cond_row(i):
    return jnp.where(i < N_CTX_TILES, 0, 1 + (i - N_CTX_TILES) // TILES_PER_SMP)


def _mod_kernel(c_ref, w_ref, b_ref, o_ref):
    c = c_ref[...]
    s = c * jax.nn.sigmoid(c)
    o_ref[...] = _mm_hi(s, w_ref[...]) + b_ref[...]


def _mod_call(cond8, w_mod, b_mod):
    tn = 1536
    return pl.pallas_call(
        _mod_kernel,
        grid=(DEPTH, 6 * D // tn),
        in_specs=[
            pl.BlockSpec((8, D), lambda l, j: (0, 0)),
            pl.BlockSpec((None, D, tn), lambda l, j: (l, 0, j)),
            pl.BlockSpec((None, 1, tn), lambda l, j: (l, 0, j)),
        ],
        out_specs=pl.BlockSpec((None, 8, tn), lambda l, j: (l, 0, j)),
        out_shape=jax.ShapeDtypeStruct((DEPTH, 8, 6 * D), F32),
        compiler_params=_params("parallel", "parallel"),
        name="mod",
    )(cond8, w_mod, b_mod.reshape(DEPTH, 1, 6 * D))


def _mod_spec(l, chunk):
    return pl.BlockSpec((None, None, 1, D), lambda i: (l, _cond_row(i), 0, chunk))


def _gain_spec(l):
    return pl.BlockSpec((None, 1, D), lambda i: (l, 0, 0))


def _pre_kernel(x_ref, g_ref, sh_ref, sc_ref, h_ref):
    h = _rms(x_ref[...], g_ref[...]) * (1.0 + sc_ref[...]) + sh_ref[...]
    h_ref[...] = h.astype(BF)


def _pre_call(x, gain3, mod4, l, sh_chunk, sc_chunk):
    return pl.pallas_call(
        _pre_kernel,
        grid=(N_TILES,),
        in_specs=[
            pl.BlockSpec((TM, D), lambda i: (i, 0)),
            _gain_spec(l),
            _mod_spec(l, sh_chunk),
            _mod_spec(l, sc_chunk),
        ],
        out_specs=pl.BlockSpec((TM, D), lambda i: (i, 0)),
        out_shape=jax.ShapeDtypeStruct((T_ALL, D), BF),
        compiler_params=_params("parallel"),
        name="prenorm",
    )(x, gain3, mod4, mod4)


GATE_W = 4 * NH
N_PLAIN = 8
N_SHIFT = 15
J_GATE = N_PLAIN + N_SHIFT
J_AKAV = N_PLAIN + 8
IN_TOTAL = N_PLAIN * PROJ_TN + GATE_W + N_SHIFT * PROJ_TN


def _proj_src(j):
    sub = 8
    return sub * jnp.where(j < N_PLAIN, j * (PROJ_TN // sub),
                           jnp.where(j < J_GATE, j * (PROJ_TN // sub) + GATE_W // sub, N_PLAIN * PROJ_TN // sub))


def _proj_dst(j):
    return jnp.where(j < J_AKAV, j, jnp.where(j == J_AKAV, CB_AK // 4, jnp.where(j < J_GATE, j - 1, J_GATE)))


def _proj_kernel(h_ref, wt_ref, o_ref, w_s):
    j = pl.program_id(0)
    w_s[...] = wt_ref[0].astype(BF)

    def rows(width):
        def body(r, carry):
            r0 = pl.multiple_of(r * ROW_CHUNK, ROW_CHUNK)
            o_ref[pl.ds(r0, ROW_CHUNK), :width] = _mm_nt(h_ref[pl.ds(r0, ROW_CHUNK), :], w_s[:width, :])
            return carry
        lax.fori_loop(0, T_ALL // ROW_CHUNK, body, 0)

    @pl.when(j < J_GATE)
    def _():
        rows(PROJ_TN)

    @pl.when(j == J_GATE)
    def _():
        rows(LANE)


def _proj_call(h, w_t, l):
    assert w_t.shape == (DEPTH, IN_TOTAL, D)
    return pl.pallas_call(
        _proj_kernel,
        grid=(J_GATE + 1,),
        in_specs=[
            pl.BlockSpec((T_ALL, D), lambda j: (0, 0), pipeline_mode=pl.Buffered(1)),
            pl.BlockSpec((pl.Element(1), pl.Element(PROJ_TN), pl.Element(D)), lambda j: (l, _proj_src(j), 0)),
        ],
        out_specs=pl.BlockSpec((T_ALL, PROJ_TN), lambda j: (0, _proj_dst(j))),
        out_shape=jax.ShapeDtypeStruct((T_ALL, PROJ_W), F32),
        scratch_shapes=[pltpu.VMEM((PROJ_TN, D), BF)],
        compiler_params=_params("arbitrary"),
        name="inproj",
    )(h, w_t)


HG = 4
DN_GROUP = 8


def _dn_kernel(*refs, L, has_s0, has_prev, emit_state):
    refs = list(refs)
    dq_ref, dk_ref, dv_ref, dg_ref, gt_ref, wq_ref, wk_ref, wv_ref, alog_ref, dtb_ref, gdn_ref = refs[:11]
    pos = 11
    s0_ref = None
    if has_s0:
        s0_ref = refs[pos]
        pos += 1
    if has_prev:
        pos += 1
    o_ref = refs[pos]
    pos += 1
    sfin_ref = None
    if emit_state:
        sfin_ref = refs[pos]
        pos += 1
    a_s, b_s, c_s, gc_s, bt_s, u_s, wq_s, qkkd_s, ge_s, s_s = refs[pos:]

    head0 = pl.program_id(1) * HG
    n_chunks = L // CHUNK
    W = HG * HD
    row = lax.broadcasted_iota(jnp.int32, (L, W), 0)

    def conv_silu(x_ref, w_ref):
        x = x_ref[...]
        w = w_ref[...]
        xm = jnp.where(row == 0, 0.0, pltpu.roll(x, 1, 0))
        xp = jnp.where(row == L - 1, 0.0, pltpu.roll(x, L - 1, 0))
        y = xm * w[0:1] + x * w[1:2] + xp * w[2:3]
        return y * jax.nn.sigmoid(y)

    def l2n(x):
        return x * lax.rsqrt(jnp.sum(x * x, axis=-1, keepdims=True) + EPS)

    q = conv_silu(dq_ref, wq_ref)
    k = conv_silu(dk_ref, wk_ref)
    for h in range(HG):
        hs = slice(h * HD, (h + 1) * HD)
        a_s[:, hs] = l2n(q[:, hs]) * (HD ** -0.5)
        b_s[:, hs] = l2n(k[:, hs])
    c_s[...] = conv_silu(dv_ref, wv_ref)

    grow = lax.broadcasted_iota(jnp.int32, (L, LANE), 0)
    glane = lax.broadcasted_iota(jnp.int32, (L, LANE), 1)
    cpos = grow & (CHUNK - 1)
    gates = gt_ref[...]
    bt_s[...] = jax.nn.sigmoid(gates)
    z = gates + dtb_ref[...]
    softplus = jnp.maximum(z, 0.0) + jnp.log(1.0 + jnp.exp(-jnp.abs(z)))
    g_all = -jnp.exp(alog_ref[...]) * softplus
    prefix, suffix = g_all, g_all
    for s in (1, 2, 4, 8, 16, 32):
        prefix = prefix + jnp.where(cpos >= s, pltpu.roll(prefix, s, 0), 0.0)
        suffix = suffix + jnp.where(cpos < CHUNK - s, pltpu.roll(suffix, L - s, 0), 0.0)
    gc_s[...] = jnp.where(glane < 3 * NH, prefix, suffix)

    ii = lax.broadcasted_iota(jnp.int32, (CHUNK, 2 * CHUNK), 0)
    jj = lax.broadcasted_iota(jnp.int32, (CHUNK, 2 * CHUNK), 1) & (CHUNK - 1)
    clane = lax.broadcasted_iota(jnp.int32, (CHUNK, LANE), 1)

    def column(arr, c):
        picked = jnp.sum(jnp.where(clane == c, arr, 0.0), axis=1, keepdims=True)
        return jnp.broadcast_to(picked, (CHUNK, HD))

    chains = [(h, d) for h in range(HG) for d in range(2)]

    def pre_body(n, carry):
        r0 = pl.multiple_of(n * CHUNK, CHUNK)
        gcr = gc_s[pl.ds(r0, CHUNK), :]
        btr = bt_s[pl.ds(r0, CHUNK), :]
        qcs = [a_s[pl.ds(r0, CHUNK), h * HD:(h + 1) * HD] for h in range(HG)]
        kcs = [b_s[pl.ds(r0, CHUNK), h * HD:(h + 1) * HD] for h in range(HG)]
        raws = []
        for h in range(HG):
            lh, ll = _split(jnp.concatenate([kcs[h], qcs[h]], axis=0))
            rh, rl = _split(jnp.concatenate([kcs[h], kcs[h]], axis=0))
            raws.append(_mm_nt(jnp.concatenate([lh, lh, ll], axis=1), jnp.concatenate([rh, rl, rh], axis=1)))
        for g0 in range(0, len(chains), DN_GROUP):
            pre_group(n, r0, gcr, btr, qcs, kcs, raws, chains[g0:g0 + DN_GROUP])
        return carry

    def pre_group(n, r0, gcr, btr, qcs, kcs, raws, group):
        a_l, x_l = [], []
        for h, d in group:
            incl = (ii >= jj) if d == 0 else (ii <= jj)
            strict = (ii > jj) if d == 0 else (ii < jj)
            col = d * NH + head0 + h
            bb = column(btr, col)
            gcb = column(gcr, 2 * NH + col)
            gct = jnp.transpose(jnp.concatenate([gcb, gcb], axis=0))[:CHUNK, :]
            decay = jnp.where(incl, jnp.exp(jnp.where(incl, gcb - gct, 0.0)), 0.0)
            eg = jnp.exp(gcb)
            gend = gcb[CHUNK - 1:CHUNK, :] if d == 0 else gcb[0:1, :]
            kc = kcs[h]
            vc = c_s[pl.ds(r0, CHUNK), h * HD:(h + 1) * HD]
            wq_s[d, h, n, CHUNK:, :] = (qcs[h] * eg).astype(BF)
            qkkd_s[d, h, n, 0:CHUNK, :] = (raws[h][CHUNK:, :CHUNK] * decay[:, :CHUNK]).astype(BF)
            qkkd_s[d, h, n, CHUNK:, :] = jnp.transpose(kc * jnp.exp(gend - gcb)).astype(BF)
            ge_s[d, h, n] = jnp.broadcast_to(jnp.exp(gend), (8, HD))
            a_l.append(jnp.where(strict, bb * raws[h][:CHUNK] * decay, 0.0))
            x_l.append(jnp.concatenate([vc * bb, kc * (bb * eg)], axis=1))
        p_l = a_l
        for level in range(6):
            ax_l, pp_l = [], []
            for p, x in zip(p_l, x_l):
                ph, pw = _split(p)
                xh, xw = _split(x)
                lhs = jnp.concatenate([ph, pw[:, :CHUNK]], axis=1)
                ax_l.append(jnp.dot(lhs, jnp.concatenate([xh, xw, xh], axis=0), preferred_element_type=F32))
                if level < 5:
                    pp_l.append(jnp.dot(lhs, jnp.concatenate([ph, pw, ph], axis=0), preferred_element_type=F32))
            p_l = pp_l
            x_l = [x - ax if level == 0 else x + ax for x, ax in zip(x_l, ax_l)]
        for (h, d), x in zip(group, x_l):
            u_s[d, h, pl.ds(r0, CHUNK), :] = x[:, :HD]
            wq_s[d, h, n, 0:CHUNK, :] = x[:, HD:].astype(BF)

    lax.fori_loop(0, n_chunks, pre_body, 0, unroll=2)

    for h, d in chains:
        s_s[d, h] = s0_ref[d, h] if has_s0 else jnp.zeros((HD, HD), F32)

    def scan_body(n, carry):
        cs = [n, n_chunks - 1 - n]
        r0s = [pl.multiple_of(c * CHUNK, CHUNK) for c in cs]
        s_l = [s_s[d, h] for h, d in chains]
        ws_l = [jnp.dot(wq_s[d, h, cs[d]], s.astype(BF), preferred_element_type=F32)
                for (h, d), s in zip(chains, s_l)]
        v_l = [u_s[d, h, pl.ds(r0s[d], CHUNK), :] - ws[:CHUNK] for (h, d), ws in zip(chains, ws_l)]
        kv_l = [jnp.dot(qkkd_s[d, h, cs[d]], v.astype(BF), preferred_element_type=F32)
                for (h, d), v in zip(chains, v_l)]
        for (h, d), s, ws, kv in zip(chains, s_l, ws_l, kv_l):
            s_s[d, h] = s * ge_s[d, h, cs[d]][0:1] + kv[CHUNK:]
            out_s = a_s if d == 0 else b_s
            out_s[pl.ds(r0s[d], CHUNK), h * HD:(h + 1) * HD] = ws[CHUNK:] + kv[:CHUNK]
        return carry

    lax.fori_loop(0, n_chunks, scan_body, 0)

    if emit_state:
        for d in range(2):
            for h in range(HG):
                sfin_ref[d, h] = s_s[d, h]

    gdn = gdn_ref[...]
    for h in range(HG):
        hs = slice(h * HD, (h + 1) * HD)
        dg = dg_ref[:, hs]
        o_ref[:, hs] = (_rms(a_s[:, hs] + b_s[:, hs], gdn) * (dg * jax.nn.sigmoid(dg))).astype(BF)


def _dn_call(proj, w_conv, alog_l, dtb_l, gdn, state, l, *, n_batch, L, row_blk0, emit_state, prev_states=None):
    has_s0 = state is not None
    has_prev = prev_states is not None
    n_chunks = L // CHUNK
    W = HG * HD

    def pspec(cb):
        return pl.BlockSpec((L, W), lambda b, j: (row_blk0 + b, cb // HG + j))

    def wspec(off):
        return pl.BlockSpec((None, 3, W), lambda b, j: (l, 0, off // HG + j))

    vec = pl.BlockSpec((1, HD), lambda b, j: (0, 0))
    in_specs = [pspec(CB_DQ), pspec(CB_DK), pspec(CB_DV), pspec(CB_DG),
                pl.BlockSpec((L, LANE), lambda b, j: (row_blk0 + b, CB_GATE)),
                wspec(0), wspec(NH), wspec(2 * NH), vec, vec, vec]
    args = [proj, proj, proj, proj, proj, w_conv, w_conv, w_conv, alog_l, dtb_l, gdn]
    if has_s0:
        in_specs.append(pl.BlockSpec((None, None, 2, HG, HD, HD), lambda b, j: (b, l, 0, j, 0, 0)))
        args.append(state)
    aliases = {}
    if has_prev:
        aliases = {len(args): 1}
        in_specs.append(pl.BlockSpec(memory_space=pl.ANY))
        args.append(prev_states)
    out_specs = [pl.BlockSpec((L, W), lambda b, j: (b, j))]
    out_shape = [jax.ShapeDtypeStruct((n_batch * L, D), BF)]
    if emit_state:
        out_specs.append(pl.BlockSpec((None, None, 2, HG, HD, HD), lambda b, j: (b, l, 0, j, 0, 0)))
        out_shape.append(jax.ShapeDtypeStruct((n_batch, DEPTH, 2, NH, HD, HD), F32))
    return pl.pallas_call(
        functools.partial(_dn_kernel, L=L, has_s0=has_s0, has_prev=has_prev, emit_state=emit_state),
        grid=(n_batch, NH // HG),
        in_specs=in_specs,
        out_specs=out_specs,
        out_shape=out_shape,
        input_output_aliases=aliases,
        scratch_shapes=[
            pltpu.VMEM((L, W), F32), pltpu.VMEM((L, W), F32), pltpu.VMEM((L, W), F32),
            pltpu.VMEM((L, LANE), F32), pltpu.VMEM((L, LANE), F32),
            pltpu.VMEM((2, HG, L, HD), F32),
            pltpu.VMEM((2, HG, n_chunks, 2 * CHUNK, HD), BF),
            pltpu.VMEM((2, HG, n_chunks, CHUNK + HD, CHUNK), BF),
            pltpu.VMEM((2, HG, n_chunks, 8, HD), F32),
            pltpu.VMEM((2, HG, HD, HD), F32),
        ],
        compiler_params=_params("parallel", "parallel"),
        name="deltanet_ctx" if not has_s0 else "deltanet_smp",
    )(*args)


def _sc_kernel(sb_ref, scg_ref, sx_ref, w_ref, o_ref, *, L):
    tn = o_ref.shape[1]
    row = lax.broadcasted_iota(jnp.int32, (L, tn), 0)
    x = scg_ref[...] * sx_ref[...]
    w = w_ref[...]
    xm = jnp.where(row == 0, 0.0, pltpu.roll(x, 1, 0))
    xp = jnp.where(row == L - 1, 0.0, pltpu.roll(x, L - 1, 0))
    y = xm * w[0:1] + x * w[1:2] + xp * w[2:3]
    o_ref[...] = (sb_ref[...] * y).astype(BF)


def _sc_call(proj, w_conv_sc, l, *, n_batch, L, row_blk0):
    tn = 256

    def pspec(cb):
        return pl.BlockSpec((L, tn), lambda b, j: (row_blk0 + b, cb * LANE // tn + j))

    return pl.pallas_call(
        functools.partial(_sc_kernel, L=L),
        grid=(n_batch, D // tn),
        in_specs=[pspec(CB_SB), pspec(CB_SCG), pspec(CB_SX),
                  pl.BlockSpec((None, 3, tn), lambda b, j: (l, 0, j))],
        out_specs=pl.BlockSpec((L, tn), lambda b, j: (b, j)),
        out_shape=jax.ShapeDtypeStruct((n_batch * L, D), BF),
        compiler_params=_params("parallel", "parallel"),
        name="shortconv",
    )(proj, proj, proj, w_conv_sc)


def _rope(x, cos, sin_signed):
    lane = lax.broadcasted_iota(jnp.int32, x.shape, 1)
    swapped = jnp.where((lane & 63) < 32, pltpu.roll(x, HD - 32, 1), pltpu.roll(x, 32, 1))
    return x * cos + swapped * sin_signed


def _softmax_pv(s_parts, v_parts):
    m = s_parts[0].max(axis=-1, keepdims=True)
    for s in s_parts[1:]:
        m = jnp.maximum(m, s.max(axis=-1, keepdims=True))
    den = None
    acc = None
    for s, v in zip(s_parts, v_parts):
        p = jnp.exp(s - m)
        ps = jnp.sum(p, axis=-1, keepdims=True)
        pv = _mm(p, v)
        den = ps if den is None else den + ps
        acc = pv if acc is None else acc + pv
    return acc / den


def _att_ctx_kernel(q_ref, k_ref, v_ref, gq_ref, gk_ref, *rest):
    o_ref, ko_ref, vo_ref = rest[-3:]
    k = _rms(k_ref[...], gk_ref[...])
    v = v_ref[...]
    ko_ref[...] = k
    vo_ref[...] = v
    gq = gq_ref[...] * (HD ** -0.5)
    q = jnp.concatenate([_rms(q_ref[:, g * HD:(g + 1) * HD], gq) for g in range(GQ)], axis=0)
    o = _softmax_pv([_mm_nt(q, k)], [v])
    for g in range(GQ):
        o_ref[:, g * HD:(g + 1) * HD] = o[g * SEQ:(g + 1) * SEQ].astype(BF)


def _att_ctx_call(proj, g_q, g_k, l, caches):
    vec = pl.BlockSpec((1, HD), lambda b, g: (0, 0))
    cache_spec = pl.BlockSpec((None, None, SEQ, HD), lambda b, g: (b, l, 0, g))
    cache_shape = jax.ShapeDtypeStruct((BATCH, DEPTH, SEQ, KVH * HD), F32)
    in_specs = [
        pl.BlockSpec((SEQ, GQ * HD), lambda b, g: (b, CB_AQ // GQ + g)),
        pl.BlockSpec((SEQ, HD), lambda b, g: (b, CB_AK + g)),
        pl.BlockSpec((SEQ, HD), lambda b, g: (b, CB_AV + g)),
        vec, vec,
    ]
    args = [proj, proj, proj, g_q, g_k]
    aliases = {}
    if caches is not None:
        aliases = {len(args): 1, len(args) + 1: 2}
        in_specs += [pl.BlockSpec(memory_space=pl.ANY)] * 2
        args += list(caches)
    return pl.pallas_call(
        _att_ctx_kernel,
        grid=(BATCH, KVH),
        in_specs=in_specs,
        out_specs=[pl.BlockSpec((SEQ, GQ * HD), lambda b, g: (b, g)), cache_spec, cache_spec],
        out_shape=[jax.ShapeDtypeStruct((T_CTX, D), BF), cache_shape, cache_shape],
        input_output_aliases=aliases,
        compiler_params=_params("parallel", "parallel"),
        name="attention_ctx",
    )(*args)


QB = 256


def _att_smp_kernel(q_ref, k_ref, v_ref, ck_ref, cv_ref, gq_ref, gk_ref, cosq_ref, sinq_ref, cosk_ref, sink_ref, o_ref):
    k = _rope(_rms(k_ref[...], gk_ref[...]), cosk_ref[...], sink_ref[...])
    gq = gq_ref[...] * (HD ** -0.5)
    cq, sq = cosq_ref[...], sinq_ref[...]
    q = jnp.concatenate([_rope(_rms(q_ref[:, g * HD:(g + 1) * HD], gq), cq, sq) for g in range(GQ)], axis=0)
    o = _softmax_pv([_mm_nt(q, ck_ref[...]), _mm_nt(q, k)], [cv_ref[...], v_ref[...]])
    for g in range(GQ):
        o_ref[:, g * HD:(g + 1) * HD] = o[g * QB:(g + 1) * QB].astype(BF)


def _att_smp_call(proj, cache_k4, cache_v4, g_q, g_k, cos_t, sin_t, l):
    nq = DEC_SEQ // QB
    smp_blk0 = T_CTX // DEC_SEQ
    q_blk0 = T_CTX // QB
    vec = pl.BlockSpec((1, HD), lambda b, g, i: (0, 0))
    tq = pl.BlockSpec((QB, HD), lambda b, g, i: (i, 0))
    tk = pl.BlockSpec((DEC_SEQ, HD), lambda b, g, i: (0, 0))
    cache = pl.BlockSpec((None, None, PAST, HD), lambda b, g, i: (b, l, 0, g))
    return pl.pallas_call(
        _att_smp_kernel,
        grid=(DEC_BATCH, KVH, nq),
        in_specs=[
            pl.BlockSpec((QB, GQ * HD), lambda b, g, i: (q_blk0 + b * nq + i, CB_AQ // GQ + g)),
            pl.BlockSpec((DEC_SEQ, HD), lambda b, g, i: (smp_blk0 + b, CB_AK + g)),
            pl.BlockSpec((DEC_SEQ, HD), lambda b, g, i: (smp_blk0 + b, CB_AV + g)),
            cache, cache, vec, vec, tq, tq, tk, tk,
        ],
        out_specs=pl.BlockSpec((QB, GQ * HD), lambda b, g, i: (b * nq + i, g)),
        out_shape=jax.ShapeDtypeStruct((T_SMP, D), BF),
        compiler_params=_params("parallel", "parallel", "parallel"),
        name="attention_smp",
    )(proj, proj, proj, cache_k4, cache_v4, g_q, g_k, cos_t, sin_t, cos_t, sin_t)


def _pick(i, ctx_ref, smp_ref):
    return jnp.where(i < N_CTX_TILES, ctx_ref[...], smp_ref[...])


def _mix_kernel(x_ref, ga_ref, gb_ref, gc_ref, ac_ref, as_ref, bc_ref, bs_ref, cc_ref, cs_ref,
                wa_ref, wb_ref, wc_ref, wo_ref, gpost_ref, gt_ref, o_ref):
    i = pl.program_id(0)
    m = jax.nn.sigmoid(ga_ref[...]) * jnp.dot(_pick(i, ac_ref, as_ref), wa_ref[...], preferred_element_type=F32)
    m += jax.nn.sigmoid(gb_ref[...]) * jnp.dot(_pick(i, bc_ref, bs_ref), wb_ref[...], preferred_element_type=F32)
    m += jax.nn.sigmoid(gc_ref[...]) * jnp.dot(_pick(i, cc_ref, cs_ref), wc_ref[...], preferred_element_type=F32)
    mix = jnp.dot(m.astype(BF), wo_ref[...], preferred_element_type=F32)
    o_ref[...] = x_ref[...] + gt_ref[...] * _rms(mix, gpost_ref[...])


def _resident(shape, index_map):
    return pl.BlockSpec(shape, index_map, pipeline_mode=pl.Buffered(1))


def _mix_call(x, proj, branches, weights, gain3, mod4, l):
    ctx_spec = pl.BlockSpec((TM, D), lambda i: (jnp.minimum(i, N_CTX_TILES - 1), 0))
    smp_spec = pl.BlockSpec((TM, D), lambda i: (jnp.maximum(i - N_CTX_TILES, 0), 0))
    wspec = _resident((None, D, D), lambda i: (l, 0, 0))

    def gate_spec(k):
        return pl.BlockSpec((TM, D), lambda i: (i, CB_MG // NH + k))

    in_specs = [pl.BlockSpec((TM, D), lambda i: (i, 0)), gate_spec(0), gate_spec(1), gate_spec(2)]
    args = [x, proj, proj, proj]
    for ctx, smp in branches:
        in_specs += [ctx_spec, smp_spec]
        args += [ctx, smp]
    in_specs += [wspec] * 4 + [_gain_spec(l), _mod_spec(l, 2)]
    args += list(weights) + [gain3, mod4]
    return pl.pallas_call(
        _mix_kernel,
        grid=(N_TILES,),
        in_specs=in_specs,
        out_specs=pl.BlockSpec((TM, D), lambda i: (i, 0)),
        out_shape=jax.ShapeDtypeStruct((T_ALL, D), F32),
        compiler_params=_params("parallel"),
        name="merge_outproj",
    )(*args)


def _mlp_kernel(x_ref, gpre_ref, sh_ref, sc_ref, wu_ref, wd_ref, gpost_ref, gt_ref, *o_refs):
    x = x_ref[...]
    h = _rms(x, gpre_ref[...]) * (1.0 + sc_ref[...]) + sh_ref[...]
    u = jnp.maximum(jnp.dot(h.astype(BF), wu_ref[...], preferred_element_type=F32), 0.0)
    dn = jnp.dot((u * u).astype(BF), wd_ref[...], preferred_element_type=F32)
    y = x + gt_ref[...] * _rms(dn, gpost_ref[...])
    if len(o_refs) == 1:
        o_refs[0][...] = y
    else:
        i = pl.program_id(0)

        @pl.when(i < N_CTX_TILES)
        def _():
            o_refs[0][...] = y

        @pl.when(i >= N_CTX_TILES)
        def _():
            o_refs[1][...] = y


def _mlp_call(x, w_up, w_down, gpre3, gpost3, mod4, l, split_out):
    if split_out:
        out_specs = [pl.BlockSpec((TM, D), lambda i: (jnp.minimum(i, N_CTX_TILES - 1), 0)),
                     pl.BlockSpec((TM, D), lambda i: (jnp.maximum(i - N_CTX_TILES, 0), 0))]
        out_shape = [jax.ShapeDtypeStruct((T_CTX, D), F32), jax.ShapeDtypeStruct((T_SMP, D), F32)]
    else:
        out_specs = pl.BlockSpec((TM, D), lambda i: (i, 0))
        out_shape = jax.ShapeDtypeStruct((T_ALL, D), F32)
    return pl.pallas_call(
        _mlp_kernel,
        grid=(N_TILES,),
        in_specs=[
            pl.BlockSpec((TM, D), lambda i: (i, 0)),
            _gain_spec(l), _mod_spec(l, 3), _mod_spec(l, 4),
            _resident((None, D, MLP_H), lambda i: (l, 0, 0)),
            _resident((None, MLP_H, D), lambda i: (l, 0, 0)),
            _gain_spec(l), _mod_spec(l, 5),
        ],
        out_specs=out_specs,
        out_shape=out_shape,
        compiler_params=_params("arbitrary"),
        name="mlp",
    )(x, gpre3, mod4, mod4, w_up, w_down, gpost3, mod4)


def _rope_tables():
    t = jnp.arange(DEC_SEQ)
    n_freq = HD // 4
    freqs = ROPE_THETA ** (-jnp.arange(n_freq, dtype=F32) / n_freq)
    ang_r = (t // GRID_W).astype(F32)[:, None] * freqs
    ang_c = (t % GRID_W).astype(F32)[:, None] * freqs
    cos_t = jnp.concatenate([jnp.cos(ang_r)] * 2 + [jnp.cos(ang_c)] * 2, axis=1)
    sin_t = jnp.concatenate([-jnp.sin(ang_r), jnp.sin(ang_r), -jnp.sin(ang_c), jnp.sin(ang_c)], axis=1)
    return cos_t, sin_t


def _gate_lanes(p):
    flat = p.reshape(DEPTH, 1, 2 * NH)
    return jnp.pad(flat, ((0, 0), (0, 0), (2 * NH, LANE - 4 * NH)))


def kernel(x_prompt, x_sample, cache_k, cache_v, state_dn, c, c_ctx, w_mod, b_mod, g_pre_mix, g_post_mix, g_pre_mlp, g_post_mlp, w_in, w_conv_dn, a_log, dt_bias, g_dn_out, w_conv_sc, g_q, g_k, w_br_dn, w_br_sc, w_br_att, w_out, w_mlp_up, w_mlp_down):
    x = jnp.concatenate([x_prompt.reshape(T_CTX, D), x_sample.reshape(T_SMP, D)], axis=0)

    cond8 = jnp.zeros((8, D), F32).at[0].set(c_ctx).at[1:1 + DEC_BATCH].set(c)
    mod4 = _mod_call(cond8, w_mod, b_mod).reshape(DEPTH, 8, 1, 6 * D)

    w_a, w_b, w_c, w_o = (w.astype(BF) for w in (w_br_dn, w_br_sc, w_br_att, w_out))
    w_up, w_down = w_mlp_up.astype(BF), w_mlp_down.astype(BF)

    gpm3, gqm3 = g_pre_mix.reshape(DEPTH, 1, D), g_post_mix.reshape(DEPTH, 1, D)
    gpl3, gql3 = g_pre_mlp.reshape(DEPTH, 1, D), g_post_mlp.reshape(DEPTH, 1, D)
    alog_p, dtb_p = _gate_lanes(a_log), _gate_lanes(dt_bias)
    cos_t, sin_t = _rope_tables()
    cache_k4 = cache_k.reshape(DEC_BATCH, DEPTH, PAST, KVH * HD)
    cache_v4 = cache_v.reshape(DEC_BATCH, DEPTH, PAST, KVH * HD)

    w_t = jnp.swapaxes(w_in, 1, 2)

    caches, new_s = None, None
    for l in range(DEPTH):
        h = _pre_call(x, gpm3, mod4, l, 0, 1)
        proj = _proj_call(h, w_t, l)

        gdn = g_dn_out[l].reshape(1, HD)
        dn_ctx, new_s = _dn_call(proj, w_conv_dn, alog_p[l], dtb_p[l], gdn, None, l,
                                 n_batch=BATCH, L=SEQ, row_blk0=0, emit_state=True, prev_states=new_s)
        (dn_smp,) = _dn_call(proj, w_conv_dn, alog_p[l], dtb_p[l], gdn, state_dn, l,
                             n_batch=DEC_BATCH, L=DEC_SEQ, row_blk0=T_CTX // DEC_SEQ, emit_state=False)
        sc_ctx = _sc_call(proj, w_conv_sc, l, n_batch=BATCH, L=SEQ, row_blk0=0)
        sc_smp = _sc_call(proj, w_conv_sc, l, n_batch=DEC_BATCH, L=DEC_SEQ, row_blk0=T_CTX // DEC_SEQ)
        gq, gk = g_q[l].reshape(1, HD), g_k[l].reshape(1, HD)
        at_ctx, *caches = _att_ctx_call(proj, gq, gk, l, caches)
        at_smp = _att_smp_call(proj, cache_k4, cache_v4, gq, gk, cos_t, sin_t, l)

        x = _mix_call(x, proj, [(dn_ctx, dn_smp), (sc_ctx, sc_smp), (at_ctx, at_smp)],
                      (w_a, w_b, w_c, w_o), gqm3, mod4, l)
        x = _mlp_call(x, w_up, w_down, gpl3, gql3, mod4, l, split_out=(l == DEPTH - 1))

    y_ctx, y_smp = x
    new_k, new_v = (t.reshape(BATCH, DEPTH, SEQ, KVH, HD) for t in caches)
    return (y_ctx.reshape(BATCH, SEQ, D), y_smp.reshape(DEC_BATCH, DEC_SEQ, D), new_k, new_v, new_s)
```

```python
import functools

import jax
import jax.numpy as jnp
from jax import lax
from jax.experimental import pallas as pl
from jax.experimental.pallas import tpu as pltpu

D = 1024
BATCH, SEQ = 16, 256
DEC_BATCH, DEC_SEQ = 2, 1024
DEPTH = 4
PAST = 256
GRID_W = 64
NH = 8
HD = 128
KVH = 2
GQ = NH // KVH
CHUNK = 64
MLP_H = 4 * D
EPS = 1e-6
ROPE_THETA = 10000.0

T_CTX = BATCH * SEQ
T_SMP = DEC_BATCH * DEC_SEQ
T_ALL = T_CTX + T_SMP
TM = 512
N_CTX_TILES = T_CTX // TM
N_TILES = T_ALL // TM
TILES_PER_SMP = DEC_SEQ // TM

LANE = 128
CB_DQ, CB_DK, CB_DV, CB_DG, CB_SB, CB_SCG, CB_SX, CB_AQ, CB_MG, CB_AK, CB_AV, CB_GATE = 0, 8, 16, 24, 32, 40, 48, 56, 64, 88, 90, 92
N_CB = 96
PROJ_W = N_CB * LANE
PROJ_TN = 4 * LANE
ROW_CHUNK = 2048

VMEM_LIMIT = 56 * 1024 * 1024

BF = jnp.bfloat16
F32 = jnp.float32
HIGHEST = lax.Precision.HIGHEST


def _params(*sem):
    return pltpu.CompilerParams(dimension_semantics=sem, vmem_limit_bytes=VMEM_LIMIT)


def _mm(a, b):
    return jnp.dot(a.astype(BF), b.astype(BF), preferred_element_type=F32)


def _mm_nt(a, b, precision=None):
    if precision is None:
        a, b = a.astype(BF), b.astype(BF)
    return lax.dot_general(a, b, (((1,), (1,)), ((), ())), precision=precision, preferred_element_type=F32)


def _split(x):
    hi = x.astype(BF)
    return hi, (x - hi.astype(F32)).astype(BF)


def _mm_hi(a, b):
    return jnp.dot(a, b, precision=HIGHEST, preferred_element_type=F32)


def _rms(x, g):
    return x * lax.rsqrt(jnp.mean(x * x, axis=-1, keepdims=True) + EPS) * g


def _cond_row(i):
    return jnp.where(i < N_CTX_TILES, 0, 1 + (i - N_CTX_TILES) // TILES_PER_SMP)


def _mod_kernel(c_ref, w_ref, b_ref, o_ref):
    c = c_ref[...]
    s = c * jax.nn.sigmoid(c)
    o_ref[...] = _mm_hi(s, w_ref[...]) + b_ref[...]


def _mod_call(cond8, w_mod, b_mod):
    tn = 1536
    return pl.pallas_call(
        _mod_kernel,
        grid=(DEPTH, 6 * D // tn),
        in_specs=[
            pl.BlockSpec((8, D), lambda l, j: (0, 0)),
            pl.BlockSpec((None, D, tn), lambda l, j: (l, 0, j)),
            pl.BlockSpec((None, 1, tn), lambda l, j: (l, 0, j)),
        ],
        out_specs=pl.BlockSpec((None, 8, tn), lambda l, j: (l, 0, j)),
        out_shape=jax.ShapeDtypeStruct((DEPTH, 8, 6 * D), F32),
        compiler_params=_params("parallel", "parallel"),
        name="mod",
    )(cond8, w_mod, b_mod.reshape(DEPTH, 1, 6 * D))


def _mod_spec(l, chunk):
    return pl.BlockSpec((None, None, 1, D), lambda i: (l, _cond_row(i), 0, chunk))


def _gain_spec(l):
    return pl.BlockSpec((None, 1, D), lambda i: (l, 0, 0))


def _pre_kernel(x_ref, g_ref, sh_ref, sc_ref, h_ref):
    h = _rms(x_ref[...], g_ref[...]) * (1.0 + sc_ref[...]) + sh_ref[...]
    h_ref[...] = h.astype(BF)


def _pre_call(x, gain3, mod4, l, sh_chunk, sc_chunk):
    return pl.pallas_call(
        _pre_kernel,
        grid=(N_TILES,),
        in_specs=[
            pl.BlockSpec((TM, D), lambda i: (i, 0)),
            _gain_spec(l),
            _mod_spec(l, sh_chunk),
            _mod_spec(l, sc_chunk),
        ],
        out_specs=pl.BlockSpec((TM, D), lambda i: (i, 0)),
        out_shape=jax.ShapeDtypeStruct((T_ALL, D), BF),
        compiler_params=_params("parallel"),
        name="prenorm",
    )(x, gain3, mod4, mod4)


GATE_W = 4 * NH
N_PLAIN = 8
N_SHIFT = 15
J_GATE = N_PLAIN + N_SHIFT
J_AKAV = N_PLAIN + 8
IN_TOTAL = N_PLAIN * PROJ_TN + GATE_W + N_SHIFT * PROJ_TN


def _proj_src(j):
    sub = 8
    return sub * jnp.where(j < N_PLAIN, j * (PROJ_TN // sub),
                           jnp.where(j < J_GATE, j * (PROJ_TN // sub) + GATE_W // sub, N_PLAIN * PROJ_TN // sub))


def _proj_dst(j):
    return jnp.where(j < J_AKAV, j, jnp.where(j == J_AKAV, CB_AK // 4, jnp.where(j < J_GATE, j - 1, J_GATE)))


def _proj_kernel(h_ref, wt_ref, o_ref, w_s):
    j = pl.program_id(0)
    w_s[...] = wt_ref[0].astype(BF)

    def rows(width):
        def body(r, carry):
            r0 = pl.multiple_of(r * ROW_CHUNK, ROW_CHUNK)
            o_ref[pl.ds(r0, ROW_CHUNK), :width] = _mm_nt(h_ref[pl.ds(r0, ROW_CHUNK), :], w_s[:width, :])
            return carry
        lax.fori_loop(0, T_ALL // ROW_CHUNK, body, 0)

    @pl.when(j < J_GATE)
    def _():
        rows(PROJ_TN)

    @pl.when(j == J_GATE)
    def _():
        rows(LANE)


def _proj_call(h, w_t, l):
    assert w_t.shape == (DEPTH, IN_TOTAL, D)
    return pl.pallas_call(
        _proj_kernel,
        grid=(J_GATE + 1,),
        in_specs=[
            pl.BlockSpec((T_ALL, D), lambda j: (0, 0), pipeline_mode=pl.Buffered(1)),
            pl.BlockSpec((pl.Element(1), pl.Element(PROJ_TN), pl.Element(D)), lambda j: (l, _proj_src(j), 0)),
        ],
        out_specs=pl.BlockSpec((T_ALL, PROJ_TN), lambda j: (0, _proj_dst(j))),
        out_shape=jax.ShapeDtypeStruct((T_ALL, PROJ_W), F32),
        scratch_shapes=[pltpu.VMEM((PROJ_TN, D), BF)],
        compiler_params=_params("arbitrary"),
        name="inproj",
    )(h, w_t)


HG = 4
DN_GROUP = 8


def _dn_kernel(*refs, L, has_s0, has_prev, emit_state):
    refs = list(refs)
    dq_ref, dk_ref, dv_ref, dg_ref, gt_ref, wq_ref, wk_ref, wv_ref, alog_ref, dtb_ref, gdn_ref = refs[:11]
    pos = 11
    s0_ref = None
    if has_s0:
        s0_ref = refs[pos]
        pos += 1
    if has_prev:
        pos += 1
    o_ref = refs[pos]
    pos += 1
    sfin_ref = None
    if emit_state:
        sfin_ref = refs[pos]
        pos += 1
    a_s, b_s, c_s, gc_s, bt_s, u_s, wq_s, qkkd_s, ge_s, s_s = refs[pos:]

    head0 = pl.program_id(1) * HG
    n_chunks = L // CHUNK
    W = HG * HD
    row = lax.broadcasted_iota(jnp.int32, (L, W), 0)

    def conv_silu(x_ref, w_ref):
        x = x_ref[...]
        w = w_ref[...]
        xm = jnp.where(row == 0, 0.0, pltpu.roll(x, 1, 0))
        xp = jnp.where(row == L - 1, 0.0, pltpu.roll(x, L - 1, 0))
        y = xm * w[0:1] + x * w[1:2] + xp * w[2:3]
        return y * jax.nn.sigmoid(y)

    def l2n(x):
        return x * lax.rsqrt(jnp.sum(x * x, axis=-1, keepdims=True) + EPS)

    q = conv_silu(dq_ref, wq_ref)
    k = conv_silu(dk_ref, wk_ref)
    for h in range(HG):
        hs = slice(h * HD, (h + 1) * HD)
        a_s[:, hs] = l2n(q[:, hs]) * (HD ** -0.5)
        b_s[:, hs] = l2n(k[:, hs])
    c_s[...] = conv_silu(dv_ref, wv_ref)

    grow = lax.broadcasted_iota(jnp.int32, (L, LANE), 0)
    glane = lax.broadcasted_iota(jnp.int32, (L, LANE), 1)
    cpos = grow & (CHUNK - 1)
    gates = gt_ref[...]
    bt_s[...] = jax.nn.sigmoid(gates)
    z = gates + dtb_ref[...]
    softplus = jnp.maximum(z, 0.0) + jnp.log(1.0 + jnp.exp(-jnp.abs(z)))
    g_all = -jnp.exp(alog_ref[...]) * softplus
    prefix, suffix = g_all, g_all
    for s in (1, 2, 4, 8, 16, 32):
        prefix = prefix + jnp.where(cpos >= s, pltpu.roll(prefix, s, 0), 0.0)
        suffix = suffix + jnp.where(cpos < CHUNK - s, pltpu.roll(suffix, L - s, 0), 0.0)
    gc_s[...] = jnp.where(glane < 3 * NH, prefix, suffix)

    ii = lax.broadcasted_iota(jnp.int32, (CHUNK, 2 * CHUNK), 0)
    jj = lax.broadcasted_iota(jnp.int32, (CHUNK, 2 * CHUNK), 1) & (CHUNK - 1)
    clane = lax.broadcasted_iota(jnp.int32, (CHUNK, LANE), 1)
    blk16 = (ii >> 4) == (jj >> 4)
    off32 = ((ii >> 4) ^ (jj >> 4)) == 1
    off64 = (ii >> 5) != (jj >> 5)

    def column(arr, c):
        picked = jnp.sum(jnp.where(clane == c, arr, 0.0), axis=1, keepdims=True)
        return jnp.broadcast_to(picked, (CHUNK, HD))

    chains = [(h, d) for h in range(HG) for d in range(2)]

    def pre_body(n, carry):
        r0 = pl.multiple_of(n * CHUNK, CHUNK)
        gcr = gc_s[pl.ds(r0, CHUNK), :]
        btr = bt_s[pl.ds(r0, CHUNK), :]
        qcs = [a_s[pl.ds(r0, CHUNK), h * HD:(h + 1) * HD] for h in range(HG)]
        kcs = [b_s[pl.ds(r0, CHUNK), h * HD:(h + 1) * HD] for h in range(HG)]
        raws = []
        for h in range(HG):
            lh, ll = _split(jnp.concatenate([kcs[h], qcs[h]], axis=0))
            rh, rl = _split(jnp.concatenate([kcs[h], kcs[h]], axis=0))
            raws.append(_mm_nt(jnp.concatenate([lh, lh, ll], axis=1), jnp.concatenate([rh, rl, rh], axis=1)))
        for g0 in range(0, len(chains), DN_GROUP):
            pre_group(n, r0, gcr, btr, qcs, kcs, raws, chains[g0:g0 + DN_GROUP])
        return carry

    def pre_group(n, r0, gcr, btr, qcs, kcs, raws, group):
        a_l, x_l = [], []
        for h, d in group:
            incl = (ii >= jj) if d == 0 else (ii <= jj)
            strict = (ii > jj) if d == 0 else (ii < jj)
            col = d * NH + head0 + h
            bb = column(btr, col)
            gcb = column(gcr, 2 * NH + col)
            gct = jnp.transpose(jnp.concatenate([gcb, gcb], axis=0))[:CHUNK, :]
            decay = jnp.where(incl, jnp.exp(jnp.where(incl, gcb - gct, 0.0)), 0.0)
            eg = jnp.exp(gcb)
            gend = gcb[CHUNK - 1:CHUNK, :] if d == 0 else gcb[0:1, :]
            kc = kcs[h]
            vc = c_s[pl.ds(r0, CHUNK), h * HD:(h + 1) * HD]
            wq_s[d, h, n, CHUNK:, :] = (qcs[h] * eg).astype(BF)
            qkkd_s[d, h, n, 0:CHUNK, :] = (raws[h][CHUNK:, :CHUNK] * decay[:, :CHUNK]).astype(BF)
            qkkd_s[d, h, n, CHUNK:, :] = jnp.transpose(kc * jnp.exp(gend - gcb)).astype(BF)
            ge_s[d, h, n] = jnp.broadcast_to(jnp.exp(gend), (8, HD))
            a_l.append(jnp.where(strict, bb * raws[h][:CHUNK] * decay, 0.0))
            x_l.append(jnp.concatenate([vc * bb, kc * (bb * eg)], axis=1))
        def each(fn, *lists):
            return [fn(*vals) for vals in zip(*lists)]

        lhs_of = lambda s: jnp.concatenate([s[0], s[1][:, :CHUNK]], axis=1)
        rhs_of = lambda s: jnp.concatenate([s[0], s[1], s[0]], axis=0)
        mm = lambda sa, sb: jnp.dot(lhs_of(sa), rhs_of(sb), preferred_element_type=F32)
        split_all = lambda ms: [_split(m) for m in ms]

        s_p = split_all([jnp.where(blk16, a, 0.0) for a in a_l])
        y_l = [jnp.where(ii == jj, 1.0, 0.0) - jnp.where(blk16, a, 0.0) for a in a_l]
        s_p = split_all(each(mm, s_p, s_p))
        for level in range(3):
            s_y = split_all(y_l)
            s_next = split_all(each(mm, s_p, s_p)) if level < 2 else None
            y_l = each(lambda y, sy, sp: y + mm(sy, sp), y_l, s_y, s_p)
            s_p = s_next
        for off_diag in (off32, off64):
            s_t = split_all(y_l)
            z_l = each(mm, s_t, split_all([jnp.where(off_diag, a, 0.0) for a in a_l]))
            y_l = each(lambda y, sz, st: y - mm(sz, st), y_l, split_all(z_l), s_t)
        x_l = each(mm, split_all(y_l), split_all(x_l))
        for (h, d), x in zip(group, x_l):
            u_s[d, h, pl.ds(r0, CHUNK), :] = x[:, :HD]
            wq_s[d, h, n, 0:CHUNK, :] = x[:, HD:].astype(BF)

    lax.fori_loop(0, n_chunks, pre_body, 0, unroll=4)

    for h, d in chains:
        s_s[d, h] = s0_ref[d, h] if has_s0 else jnp.zeros((HD, HD), F32)

    def scan_body(n, carry):
        cs = [n, n_chunks - 1 - n]
        r0s = [pl.multiple_of(c * CHUNK, CHUNK) for c in cs]
        s_l = [s_s[d, h] for h, d in chains]
        ws_l = [jnp.dot(wq_s[d, h, cs[d]], s.astype(BF), preferred_element_type=F32)
                for (h, d), s in zip(chains, s_l)]
        v_l = [u_s[d, h, pl.ds(r0s[d], CHUNK), :] - ws[:CHUNK] for (h, d), ws in zip(chains, ws_l)]
        kv_l = [jnp.dot(qkkd_s[d, h, cs[d]], v.astype(BF), preferred_element_type=F32)
                for (h, d), v in zip(chains, v_l)]
        for (h, d), s, ws, kv in zip(chains, s_l, ws_l, kv_l):
            s_s[d, h] = s * ge_s[d, h, cs[d]][0:1] + kv[CHUNK:]
            out_s = a_s if d == 0 else b_s
            out_s[pl.ds(r0s[d], CHUNK), h * HD:(h + 1) * HD] = ws[CHUNK:] + kv[:CHUNK]
        return carry

    lax.fori_loop(0, n_chunks, scan_body, 0)

    if emit_state:
        for d in range(2):
            for h in range(HG):
                sfin_ref[d, h] = s_s[d, h]

    gdn = gdn_ref[...]
    for h in range(HG):
        hs = slice(h * HD, (h + 1) * HD)
        dg = dg_ref[:, hs]
        o_ref[:, hs] = (_rms(a_s[:, hs] + b_s[:, hs], gdn) * (dg * jax.nn.sigmoid(dg))).astype(BF)


def _dn_call(proj, w_conv, alog_l, dtb_l, gdn, state, l, *, n_batch, L, row_blk0, emit_state, prev_states=None):
    has_s0 = state is not None
    has_prev = prev_states is not None
    n_chunks = L // CHUNK
    W = HG * HD

    def pspec(cb):
        return pl.BlockSpec((L, W), lambda b, j: (row_blk0 + b, cb // HG + j))

    def wspec(off):
        return pl.BlockSpec((None, 3, W), lambda b, j: (l, 0, off // HG + j))

    vec = pl.BlockSpec((1, HD), lambda b, j: (0, 0))
    in_specs = [pspec(CB_DQ), pspec(CB_DK), pspec(CB_DV), pspec(CB_DG),
                pl.BlockSpec((L, LANE), lambda b, j: (row_blk0 + b, CB_GATE)),
                wspec(0), wspec(NH), wspec(2 * NH), vec, vec, vec]
    args = [proj, proj, proj, proj, proj, w_conv, w_conv, w_conv, alog_l, dtb_l, gdn]
    if has_s0:
        in_specs.append(pl.BlockSpec((None, None, 2, HG, HD, HD), lambda b, j: (b, l, 0, j, 0, 0)))
        args.append(state)
    aliases = {}
    if has_prev:
        aliases = {len(args): 1}
        in_specs.append(pl.BlockSpec(memory_space=pl.ANY))
        args.append(prev_states)
    out_specs = [pl.BlockSpec((L, W), lambda b, j: (b, j))]
    out_shape = [jax.ShapeDtypeStruct((n_batch * L, D), BF)]
    if emit_state:
        out_specs.append(pl.BlockSpec((None, None, 2, HG, HD, HD), lambda b, j: (b, l, 0, j, 0, 0)))
        out_shape.append(jax.ShapeDtypeStruct((n_batch, DEPTH, 2, NH, HD, HD), F32))
    return pl.pallas_call(
        functools.partial(_dn_kernel, L=L, has_s0=has_s0, has_prev=has_prev, emit_state=emit_state),
        grid=(n_batch, NH // HG),
        in_specs=in_specs,
        out_specs=out_specs,
        out_shape=out_shape,
        input_output_aliases=aliases,
        scratch_shapes=[
            pltpu.VMEM((L, W), F32), pltpu.VMEM((L, W), F32), pltpu.VMEM((L, W), F32),
            pltpu.VMEM((L, LANE), F32), pltpu.VMEM((L, LANE), F32),
            pltpu.VMEM((2, HG, L, HD), F32),
            pltpu.VMEM((2, HG, n_chunks, 2 * CHUNK, HD), BF),
            pltpu.VMEM((2, HG, n_chunks, CHUNK + HD, CHUNK), BF),
            pltpu.VMEM((2, HG, n_chunks, 8, HD), F32),
            pltpu.VMEM((2, HG, HD, HD), F32),
        ],
        compiler_params=_params("parallel", "parallel"),
        name="deltanet_ctx" if not has_s0 else "deltanet_smp",
    )(*args)


def _sc_kernel(sb_ref, scg_ref, sx_ref, w_ref, o_ref, *, L):
    tn = o_ref.shape[1]
    row = lax.broadcasted_iota(jnp.int32, (L, tn), 0)
    x = scg_ref[...] * sx_ref[...]
    w = w_ref[...]
    xm = jnp.where(row == 0, 0.0, pltpu.roll(x, 1, 0))
    xp = jnp.where(row == L - 1, 0.0, pltpu.roll(x, L - 1, 0))
    y = xm * w[0:1] + x * w[1:2] + xp * w[2:3]
    o_ref[...] = (sb_ref[...] * y).astype(BF)


def _sc_call(proj, w_conv_sc, l, *, n_batch, L, row_blk0):
    tn = 256

    def pspec(cb):
        return pl.BlockSpec((L, tn), lambda b, j: (row_blk0 + b, cb * LANE // tn + j))

    return pl.pallas_call(
        functools.partial(_sc_kernel, L=L),
        grid=(n_batch, D // tn),
        in_specs=[pspec(CB_SB), pspec(CB_SCG), pspec(CB_SX),
                  pl.BlockSpec((None, 3, tn), lambda b, j: (l, 0, j))],
        out_specs=pl.BlockSpec((L, tn), lambda b, j: (b, j)),
        out_shape=jax.ShapeDtypeStruct((n_batch * L, D), BF),
        compiler_params=_params("parallel", "parallel"),
        name="shortconv",
    )(proj, proj, proj, w_conv_sc)


def _rope(x, cos, sin_signed):
    lane = lax.broadcasted_iota(jnp.int32, x.shape, 1)
    swapped = jnp.where((lane & 63) < 32, pltpu.roll(x, HD - 32, 1), pltpu.roll(x, 32, 1))
    return x * cos + swapped * sin_signed


def _softmax_pv(s_parts, v_parts):
    m = s_parts[0].max(axis=-1, keepdims=True)
    for s in s_parts[1:]:
        m = jnp.maximum(m, s.max(axis=-1, keepdims=True))
    den = None
    acc = None
    for s, v in zip(s_parts, v_parts):
        p = jnp.exp(s - m)
        ps = jnp.sum(p, axis=-1, keepdims=True)
        pv = _mm(p, v)
        den = ps if den is None else den + ps
        acc = pv if acc is None else acc + pv
    return acc / den


def _att_ctx_kernel(q_ref, k_ref, v_ref, gq_ref, gk_ref, *rest):
    o_ref, ko_ref, vo_ref = rest[-3:]
    k = _rms(k_ref[...], gk_ref[...])
    v = v_ref[...]
    ko_ref[...] = k
    vo_ref[...] = v
    gq = gq_ref[...] * (HD ** -0.5)
    kb, vb = k.astype(BF), v.astype(BF)

    def scores(g):
        return [_mm_nt(_rms(q_ref[:, g * HD:(g + 1) * HD], gq), kb)]

    s_next = scores(0)
    for g in range(GQ):
        s_cur = s_next
        if g + 1 < GQ:
            s_next = scores(g + 1)
        o_ref[:, g * HD:(g + 1) * HD] = _softmax_pv(s_cur, [vb]).astype(BF)


def _att_ctx_call(proj, g_q, g_k, l, caches):
    vec = pl.BlockSpec((1, HD), lambda b, g: (0, 0))
    cache_spec = pl.BlockSpec((None, None, SEQ, HD), lambda b, g: (b, l, 0, g))
    cache_shape = jax.ShapeDtypeStruct((BATCH, DEPTH, SEQ, KVH * HD), F32)
    in_specs = [
        pl.BlockSpec((SEQ, GQ * HD), lambda b, g: (b, CB_AQ // GQ + g)),
        pl.BlockSpec((SEQ, HD), lambda b, g: (b, CB_AK + g)),
        pl.BlockSpec((SEQ, HD), lambda b, g: (b, CB_AV + g)),
        vec, vec,
    ]
    args = [proj, proj, proj, g_q, g_k]
    aliases = {}
    if caches is not None:
        aliases = {len(args): 1, len(args) + 1: 2}
        in_specs += [pl.BlockSpec(memory_space=pl.ANY)] * 2
        args += list(caches)
    return pl.pallas_call(
        _att_ctx_kernel,
        grid=(BATCH, KVH),
        in_specs=in_specs,
        out_specs=[pl.BlockSpec((SEQ, GQ * HD), lambda b, g: (b, g)), cache_spec, cache_spec],
        out_shape=[jax.ShapeDtypeStruct((T_CTX, D), BF), cache_shape, cache_shape],
        input_output_aliases=aliases,
        compiler_params=_params("parallel", "parallel"),
        name="attention_ctx",
    )(*args)


QB = 256


def _att_smp_kernel(q_ref, k_ref, v_ref, ck_ref, cv_ref, gq_ref, gk_ref, cosq_ref, sinq_ref, cosk_ref, sink_ref, o_ref):
    k = _rope(_rms(k_ref[...], gk_ref[...]), cosk_ref[...], sink_ref[...])
    gq = gq_ref[...] * (HD ** -0.5)
    cq, sq = cosq_ref[...], sinq_ref[...]
    kb, ckb = k.astype(BF), ck_ref[...].astype(BF)
    vs = [cv_ref[...].astype(BF), v_ref[...].astype(BF)]

    def scores(g):
        q = _rope(_rms(q_ref[:, g * HD:(g + 1) * HD], gq), cq, sq).astype(BF)
        return [_mm_nt(q, ckb), _mm_nt(q, kb)]

    s_next = scores(0)
    for g in range(GQ):
        s_cur = s_next
        if g + 1 < GQ:
            s_next = scores(g + 1)
        o_ref[:, g * HD:(g + 1) * HD] = _softmax_pv(s_cur, vs).astype(BF)


def _att_smp_call(proj, cache_k4, cache_v4, g_q, g_k, cos_t, sin_t, l):
    nq = DEC_SEQ // QB
    smp_blk0 = T_CTX // DEC_SEQ
    q_blk0 = T_CTX // QB
    vec = pl.BlockSpec((1, HD), lambda b, g, i: (0, 0))
    tq = pl.BlockSpec((QB, HD), lambda b, g, i: (i, 0))
    tk = pl.BlockSpec((DEC_SEQ, HD), lambda b, g, i: (0, 0))
    cache = pl.BlockSpec((None, None, PAST, HD), lambda b, g, i: (b, l, 0, g))
    return pl.pallas_call(
        _att_smp_kernel,
        grid=(DEC_BATCH, KVH, nq),
        in_specs=[
            pl.BlockSpec((QB, GQ * HD), lambda b, g, i: (q_blk0 + b * nq + i, CB_AQ // GQ + g)),
            pl.BlockSpec((DEC_SEQ, HD), lambda b, g, i: (smp_blk0 + b, CB_AK + g)),
            pl.BlockSpec((DEC_SEQ, HD), lambda b, g, i: (smp_blk0 + b, CB_AV + g)),
            cache, cache, vec, vec, tq, tq, tk, tk,
        ],
        out_specs=pl.BlockSpec((QB, GQ * HD), lambda b, g, i: (b * nq + i, g)),
        out_shape=jax.ShapeDtypeStruct((T_SMP, D), BF),
        compiler_params=_params("parallel", "parallel", "parallel"),
        name="attention_smp",
    )(proj, proj, proj, cache_k4, cache_v4, g_q, g_k, cos_t, sin_t, cos_t, sin_t)


def _pick(i, ctx_ref, smp_ref):
    return jnp.where(i < N_CTX_TILES, ctx_ref[...], smp_ref[...])


def _mix_kernel(x_ref, ga_ref, gb_ref, gc_ref, ac_ref, as_ref, bc_ref, bs_ref, cc_ref, cs_ref,
                wa_ref, wb_ref, wc_ref, wo_ref, gpost_ref, gt_ref, o_ref):
    i = pl.program_id(0)
    m = jax.nn.sigmoid(ga_ref[...]) * jnp.dot(_pick(i, ac_ref, as_ref), wa_ref[...], preferred_element_type=F32)
    m += jax.nn.sigmoid(gb_ref[...]) * jnp.dot(_pick(i, bc_ref, bs_ref), wb_ref[...], preferred_element_type=F32)
    m += jax.nn.sigmoid(gc_ref[...]) * jnp.dot(_pick(i, cc_ref, cs_ref), wc_ref[...], preferred_element_type=F32)
    mix = jnp.dot(m.astype(BF), wo_ref[...], preferred_element_type=F32)
    o_ref[...] = x_ref[...] + gt_ref[...] * _rms(mix, gpost_ref[...])


def _resident(shape, index_map):
    return pl.BlockSpec(shape, index_map, pipeline_mode=pl.Buffered(1))


def _mix_call(x, proj, branches, weights, gain3, mod4, l):
    ctx_spec = pl.BlockSpec((TM, D), lambda i: (jnp.minimum(i, N_CTX_TILES - 1), 0))
    smp_spec = pl.BlockSpec((TM, D), lambda i: (jnp.maximum(i - N_CTX_TILES, 0), 0))
    wspec = _resident((None, D, D), lambda i: (l, 0, 0))

    def gate_spec(k):
        return pl.BlockSpec((TM, D), lambda i: (i, CB_MG // NH + k))

    in_specs = [pl.BlockSpec((TM, D), lambda i: (i, 0)), gate_spec(0), gate_spec(1), gate_spec(2)]
    args = [x, proj, proj, proj]
    for ctx, smp in branches:
        in_specs += [ctx_spec, smp_spec]
        args += [ctx, smp]
    in_specs += [wspec] * 4 + [_gain_spec(l), _mod_spec(l, 2)]
    args += list(weights) + [gain3, mod4]
    return pl.pallas_call(
        _mix_kernel,
        grid=(N_TILES,),
        in_specs=in_specs,
        out_specs=pl.BlockSpec((TM, D), lambda i: (i, 0)),
        out_shape=jax.ShapeDtypeStruct((T_ALL, D), F32),
        compiler_params=_params("parallel"),
        name="merge_outproj",
    )(*args)


def _mlp_kernel(x_ref, gpre_ref, sh_ref, sc_ref, wu_ref, wd_ref, gpost_ref, gt_ref, *o_refs):
    x = x_ref[...]
    h = _rms(x, gpre_ref[...]) * (1.0 + sc_ref[...]) + sh_ref[...]
    u = jnp.maximum(jnp.dot(h.astype(BF), wu_ref[...], preferred_element_type=F32), 0.0)
    dn = jnp.dot((u * u).astype(BF), wd_ref[...], preferred_element_type=F32)
    y = x + gt_ref[...] * _rms(dn, gpost_ref[...])
    if len(o_refs) == 1:
        o_refs[0][...] = y
    else:
        i = pl.program_id(0)

        @pl.when(i < N_CTX_TILES)
        def _():
            o_refs[0][...] = y

        @pl.when(i >= N_CTX_TILES)
        def _():
            o_refs[1][...] = y


def _mlp_call(x, w_up, w_down, gpre3, gpost3, mod4, l, split_out):
    if split_out:
        out_specs = [pl.BlockSpec((TM, D), lambda i: (jnp.minimum(i, N_CTX_TILES - 1), 0)),
                     pl.BlockSpec((TM, D), lambda i: (jnp.maximum(i - N_CTX_TILES, 0), 0))]
        out_shape = [jax.ShapeDtypeStruct((T_CTX, D), F32), jax.ShapeDtypeStruct((T_SMP, D), F32)]
    else:
        out_specs = pl.BlockSpec((TM, D), lambda i: (i, 0))
        out_shape = jax.ShapeDtypeStruct((T_ALL, D), F32)
    return pl.pallas_call(
        _mlp_kernel,
        grid=(N_TILES,),
        in_specs=[
            pl.BlockSpec((TM, D), lambda i: (i, 0)),
            _gain_spec(l), _mod_spec(l, 3), _mod_spec(l, 4),
            _resident((None, D, MLP_H), lambda i: (l, 0, 0)),
            _resident((None, MLP_H, D), lambda i: (l, 0, 0)),
            _gain_spec(l), _mod_spec(l, 5),
        ],
        out_specs=out_specs,
        out_shape=out_shape,
        compiler_params=_params("arbitrary"),
        name="mlp",
    )(x, gpre3, mod4, mod4, w_up, w_down, gpost3, mod4)


def _rope_tables():
    t = jnp.arange(DEC_SEQ)
    n_freq = HD // 4
    freqs = ROPE_THETA ** (-jnp.arange(n_freq, dtype=F32) / n_freq)
    ang_r = (t // GRID_W).astype(F32)[:, None] * freqs
    ang_c = (t % GRID_W).astype(F32)[:, None] * freqs
    cos_t = jnp.concatenate([jnp.cos(ang_r)] * 2 + [jnp.cos(ang_c)] * 2, axis=1)
    sin_t = jnp.concatenate([-jnp.sin(ang_r), jnp.sin(ang_r), -jnp.sin(ang_c), jnp.sin(ang_c)], axis=1)
    return cos_t, sin_t


def _gate_lanes(p):
    flat = p.reshape(DEPTH, 1, 2 * NH)
    return jnp.pad(flat, ((0, 0), (0, 0), (2 * NH, LANE - 4 * NH)))


def kernel(x_prompt, x_sample, cache_k, cache_v, state_dn, c, c_ctx, w_mod, b_mod, g_pre_mix, g_post_mix, g_pre_mlp, g_post_mlp, w_in, w_conv_dn, a_log, dt_bias, g_dn_out, w_conv_sc, g_q, g_k, w_br_dn, w_br_sc, w_br_att, w_out, w_mlp_up, w_mlp_down):
    x = jnp.concatenate([x_prompt.reshape(T_CTX, D), x_sample.reshape(T_SMP, D)], axis=0)

    cond8 = jnp.zeros((8, D), F32).at[0].set(c_ctx).at[1:1 + DEC_BATCH].set(c)
    mod4 = _mod_call(cond8, w_mod, b_mod).reshape(DEPTH, 8, 1, 6 * D)

    w_a, w_b, w_c, w_o = (w.astype(BF) for w in (w_br_dn, w_br_sc, w_br_att, w_out))
    w_up, w_down = w_mlp_up.astype(BF), w_mlp_down.astype(BF)

    gpm3, gqm3 = g_pre_mix.reshape(DEPTH, 1, D), g_post_mix.reshape(DEPTH, 1, D)
    gpl3, gql3 = g_pre_mlp.reshape(DEPTH, 1, D), g_post_mlp.reshape(DEPTH, 1, D)
    alog_p, dtb_p = _gate_lanes(a_log), _gate_lanes(dt_bias)
    cos_t, sin_t = _rope_tables()
    cache_k4 = cache_k.reshape(DEC_BATCH, DEPTH, PAST, KVH * HD)
    cache_v4 = cache_v.reshape(DEC_BATCH, DEPTH, PAST, KVH * HD)

    w_t = jnp.swapaxes(w_in, 1, 2)

    caches, new_s = None, None
    for l in range(DEPTH):
        h = _pre_call(x, gpm3, mod4, l, 0, 1)
        proj = _proj_call(h, w_t, l)

        gdn = g_dn_out[l].reshape(1, HD)
        dn_ctx, new_s = _dn_call(proj, w_conv_dn, alog_p[l], dtb_p[l], gdn, None, l,
                                 n_batch=BATCH, L=SEQ, row_blk0=0, emit_state=True, prev_states=new_s)
        (dn_smp,) = _dn_call(proj, w_conv_dn, alog_p[l], dtb_p[l], gdn, state_dn, l,
                             n_batch=DEC_BATCH, L=DEC_SEQ, row_blk0=T_CTX // DEC_SEQ, emit_state=False)
        sc_ctx = _sc_call(proj, w_conv_sc, l, n_batch=BATCH, L=SEQ, row_blk0=0)
        sc_smp = _sc_call(proj, w_conv_sc, l, n_batch=DEC_BATCH, L=DEC_SEQ, row_blk0=T_CTX // DEC_SEQ)
        gq, gk = g_q[l].reshape(1, HD), g_k[l].reshape(1, HD)
        at_ctx, *caches = _att_ctx_call(proj, gq, gk, l, caches)
        at_smp = _att_smp_call(proj, cache_k4, cache_v4, gq, gk, cos_t, sin_t, l)

        x = _mix_call(x, proj, [(dn_ctx, dn_smp), (sc_ctx, sc_smp), (at_ctx, at_smp)],
                      (w_a, w_b, w_c, w_o), gqm3, mod4, l)
        x = _mlp_call(x, w_up, w_down, gpl3, gql3, mod4, l, split_out=(l == DEPTH - 1))

    y_ctx, y_smp = x
    new_k, new_v = (t.reshape(BATCH, DEPTH, SEQ, KVH, HD) for t in caches)
    return (y_ctx.reshape(BATCH, SEQ, D), y_smp.reshape(DEC_BATCH, DEC_SEQ, D), new_k, new_v, new_s)
```

```python
import functools

import jax
import jax.numpy as jnp
from jax import lax
from jax.experimental import pallas as pl
from jax.experimental.pallas import tpu as pltpu

D = 1024
BATCH, SEQ = 16, 256
DEC_BATCH, DEC_SEQ = 2, 1024
DEPTH = 4
PAST = 256
GRID_W = 64
NH = 8
HD = 128
KVH = 2
GQ = NH // KVH
CHUNK = 64
MLP_H = 4 * D
EPS = 1e-6
ROPE_THETA = 10000.0

T_CTX = BATCH * SEQ
T_SMP = DEC_BATCH * DEC_SEQ
T_ALL = T_CTX + T_SMP
TM = 512
N_CTX_TILES = T_CTX // TM
N_TILES = T_ALL // TM
TILES_PER_SMP = DEC_SEQ // TM

LANE = 128
CB_DQ, CB_DK, CB_DV, CB_DG, CB_SB, CB_SCG, CB_SX, CB_AQ, CB_MG, CB_AK, CB_AV, CB_GATE = 0, 8, 16, 24, 32, 40, 48, 56, 64, 88, 90, 92
N_CB = 96
PROJ_W = N_CB * LANE
PROJ_TN = 4 * LANE
ROW_CHUNK = 2048

VMEM_LIMIT = 56 * 1024 * 1024

BF = jnp.bfloat16
F32 = jnp.float32
HIGHEST = lax.Precision.HIGHEST


def _params(*sem):
    return pltpu.CompilerParams(dimension_semantics=sem, vmem_limit_bytes=VMEM_LIMIT)


def _mm(a, b):
    return jnp.dot(a.astype(BF), b.astype(BF), preferred_element_type=F32)


def _mm_nt(a, b, precision=None):
    if precision is None:
        a, b = a.astype(BF), b.astype(BF)
    return lax.dot_general(a, b, (((1,), (1,)), ((), ())), precision=precision, preferred_element_type=F32)


def _split(x):
    hi = x.astype(BF)
    return hi, (x - hi.astype(F32)).astype(BF)


def _mm_hi(a, b):
    return jnp.dot(a, b, precision=HIGHEST, preferred_element_type=F32)


def _rms(x, g):
    return x * lax.rsqrt(jnp.mean(x * x, axis=-1, keepdims=True) + EPS) * g


def _cond_row(i):
    return jnp.where(i < N_CTX_TILES, 0, 1 + (i - N_CTX_TILES) // TILES_PER_SMP)


def _mod_kernel(c_ref, w_ref, b_ref, o_ref):
    c = c_ref[...]
    s = c * jax.nn.sigmoid(c)
    o_ref[...] = _mm_hi(s, w_ref[...]) + b_ref[...]


def _mod_call(cond8, w_mod, b_mod):
    tn = 1536
    return pl.pallas_call(
        _mod_kernel,
        grid=(DEPTH, 6 * D // tn),
        in_specs=[
            pl.BlockSpec((8, D), lambda l, j: (0, 0)),
            pl.BlockSpec((None, D, tn), lambda l, j: (l, 0, j)),
            pl.BlockSpec((None, 1, tn), lambda l, j: (l, 0, j)),
        ],
        out_specs=pl.BlockSpec((None, 8, tn), lambda l, j: (l, 0, j)),
        out_shape=jax.ShapeDtypeStruct((DEPTH, 8, 6 * D), F32),
        compiler_params=_params("parallel", "parallel"),
        name="mod",
    )(cond8, w_mod, b_mod.reshape(DEPTH, 1, 6 * D))


def _mod_spec(l, chunk):
    return pl.BlockSpec((None, None, 1, D), lambda i: (l, _cond_row(i), 0, chunk))


def _gain_spec(l):
    return pl.BlockSpec((None, 1, D), lambda i: (l, 0, 0))


def _pre_kernel(x_ref, g_ref, sh_ref, sc_ref, h_ref):
    h = _rms(x_ref[...], g_ref[...]) * (1.0 + sc_ref[...]) + sh_ref[...]
    h_ref[...] = h.astype(BF)


def _pre_call(x, gain3, mod4, l, sh_chunk, sc_chunk):
    return pl.pallas_call(
        _pre_kernel,
        grid=(N_TILES,),
        in_specs=[
            pl.BlockSpec((TM, D), lambda i: (i, 0)),
            _gain_spec(l),
            _mod_spec(l, sh_chunk),
            _mod_spec(l, sc_chunk),
        ],
        out_specs=pl.BlockSpec((TM, D), lambda i: (i, 0)),
        out_shape=jax.ShapeDtypeStruct((T_ALL, D), BF),
        compiler_params=_params("parallel"),
        name="prenorm",
    )(x, gain3, mod4, mod4)


GATE_W = 4 * NH
N_PLAIN = 8
N_SHIFT = 15
J_GATE = N_PLAIN + N_SHIFT
J_AKAV = N_PLAIN + 8
IN_TOTAL = N_PLAIN * PROJ_TN + GATE_W + N_SHIFT * PROJ_TN


def _proj_src(j):
    sub = 8
    return sub * jnp.where(j < N_PLAIN, j * (PROJ_TN // sub),
                           jnp.where(j < J_GATE, j * (PROJ_TN // sub) + GATE_W // sub, N_PLAIN * PROJ_TN // sub))


def _proj_dst(j):
    return jnp.where(j < J_AKAV, j, jnp.where(j == J_AKAV, CB_AK // 4, jnp.where(j < J_GATE, j - 1, J_GATE)))


def _proj_kernel(h_ref, wt_ref, o_ref, w_s):
    j = pl.program_id(0)
    w_s[...] = wt_ref[0].astype(BF)

    def rows(width):
        def body(r, carry):
            r0 = pl.multiple_of(r * ROW_CHUNK, ROW_CHUNK)
            o_ref[pl.ds(r0, ROW_CHUNK), :width] = _mm_nt(h_ref[pl.ds(r0, ROW_CHUNK), :], w_s[:width, :])
            return carry
        lax.fori_loop(0, T_ALL // ROW_CHUNK, body, 0)

    @pl.when(j < J_GATE)
    def _():
        rows(PROJ_TN)

    @pl.when(j == J_GATE)
    def _():
        rows(LANE)

        def zero(r, carry):
            r0 = pl.multiple_of(r * TM, TM)
            o_ref[pl.ds(r0, TM), LANE:] = jnp.zeros((TM, PROJ_TN - LANE), F32)
            return carry
        lax.fori_loop(0, T_ALL // TM, zero, 0)


def _proj_call(h, w_t, l):
    assert w_t.shape == (DEPTH, IN_TOTAL, D)
    return pl.pallas_call(
        _proj_kernel,
        grid=(J_GATE + 1,),
        in_specs=[
            pl.BlockSpec((T_ALL, D), lambda j: (0, 0), pipeline_mode=pl.Buffered(1)),
            pl.BlockSpec((pl.Element(1), pl.Element(PROJ_TN), pl.Element(D)), lambda j: (l, _proj_src(j), 0)),
        ],
        out_specs=pl.BlockSpec((T_ALL, PROJ_TN), lambda j: (0, _proj_dst(j))),
        out_shape=jax.ShapeDtypeStruct((T_ALL, PROJ_W), F32),
        scratch_shapes=[pltpu.VMEM((PROJ_TN, D), BF)],
        compiler_params=_params("arbitrary"),
        name="inproj",
    )(h, w_t)


HG = 4
DN_CHUNKS = 2
DN_UNROLL = 2


def _dn_kernel(*refs, L, has_s0, has_prev, emit_state):
    refs = list(refs)
    dq_ref, dk_ref, dv_ref, dg_ref, gt_ref, wq_ref, wk_ref, wv_ref, alog_ref, dtb_ref, gdn_ref = refs[:11]
    pos = 11
    s0_ref = None
    if has_s0:
        s0_ref = refs[pos]
        pos += 1
    if has_prev:
        pos += 1
    o_ref = refs[pos]
    pos += 1
    sfin_ref = None
    if emit_state:
        sfin_ref = refs[pos]
        pos += 1
    a_s, b_s, c_s, gc_s, bt_s, u_s, wq_s, qkkd_s, ge_s, s_s = refs[pos:]

    head0 = pl.program_id(1) * HG
    n_chunks = L // CHUNK
    W = HG * HD
    row = lax.broadcasted_iota(jnp.int32, (L, W), 0)

    def conv_silu(x_ref, w_ref):
        x = x_ref[...]
        w = w_ref[...]
        xm = jnp.where(row == 0, 0.0, pltpu.roll(x, 1, 0))
        xp = jnp.where(row == L - 1, 0.0, pltpu.roll(x, L - 1, 0))
        y = xm * w[0:1] + x * w[1:2] + xp * w[2:3]
        return y * jax.nn.sigmoid(y)

    def l2n(x):
        return x * lax.rsqrt(jnp.sum(x * x, axis=-1, keepdims=True) + EPS)

    q = conv_silu(dq_ref, wq_ref)
    k = conv_silu(dk_ref, wk_ref)
    for h in range(HG):
        hs = slice(h * HD, (h + 1) * HD)
        a_s[:, hs] = l2n(q[:, hs]) * (HD ** -0.5)
        b_s[:, hs] = l2n(k[:, hs])
    c_s[...] = conv_silu(dv_ref, wv_ref)

    grow = lax.broadcasted_iota(jnp.int32, (L, LANE), 0)
    glane = lax.broadcasted_iota(jnp.int32, (L, LANE), 1)
    cpos = grow & (CHUNK - 1)
    gates = gt_ref[...]
    bt_s[...] = jax.nn.sigmoid(gates)
    z = gates + dtb_ref[...]
    softplus = jnp.maximum(z, 0.0) + jnp.log(1.0 + jnp.exp(-jnp.abs(z)))
    g_all = -jnp.exp(alog_ref[...]) * softplus
    prefix, suffix = g_all, g_all
    for s in (1, 2, 4, 8, 16, 32):
        prefix = prefix + jnp.where(cpos >= s, pltpu.roll(prefix, s, 0), 0.0)
        suffix = suffix + jnp.where(cpos < CHUNK - s, pltpu.roll(suffix, L - s, 0), 0.0)
    gc_s[...] = jnp.where(glane < 3 * NH, prefix, suffix)

    ii = lax.broadcasted_iota(jnp.int32, (CHUNK, 2 * CHUNK), 0)
    jj = lax.broadcasted_iota(jnp.int32, (CHUNK, 2 * CHUNK), 1) & (CHUNK - 1)
    clane = lax.broadcasted_iota(jnp.int32, (CHUNK, LANE), 1)
    blk16 = (ii >> 4) == (jj >> 4)
    off32 = ((ii >> 4) ^ (jj >> 4)) == 1
    off64 = (ii >> 5) != (jj >> 5)

    def column(arr, c):
        picked = jnp.sum(jnp.where(clane == c, arr, 0.0), axis=1, keepdims=True)
        return jnp.broadcast_to(picked, (CHUNK, HD))

    chains = [(h, d) for h in range(HG) for d in range(2)]

    def pre_body(m, carry):
        group = []
        for c in range(DN_CHUNKS):
            n = m * DN_CHUNKS + c
            r0 = pl.multiple_of(n * CHUNK, CHUNK)
            gcr = gc_s[pl.ds(r0, CHUNK), :]
            btr = bt_s[pl.ds(r0, CHUNK), :]
            for h in range(HG):
                qc = a_s[pl.ds(r0, CHUNK), h * HD:(h + 1) * HD]
                kc = b_s[pl.ds(r0, CHUNK), h * HD:(h + 1) * HD]
                lh, ll = _split(jnp.concatenate([kc, qc], axis=0))
                rh, rl = _split(jnp.concatenate([kc, kc], axis=0))
                raw = _mm_nt(jnp.concatenate([lh, lh, ll], axis=1),
                             jnp.concatenate([rh, rl, rh], axis=1))
                group += [(h, d, n, r0, gcr, btr, qc, kc, raw) for d in range(2)]
        pre_group(group)
        return carry

    def pre_group(group):
        a_l, x_l = [], []
        for h, d, n, r0, gcr, btr, qc, kc, raw in group:
            incl = (ii >= jj) if d == 0 else (ii <= jj)
            strict = (ii > jj) if d == 0 else (ii < jj)
            col = d * NH + head0 + h
            bb = column(btr, col)
            gcb = column(gcr, 2 * NH + col)
            gct = jnp.transpose(jnp.concatenate([gcb, gcb], axis=0))[:CHUNK, :]
            decay = jnp.where(incl, jnp.exp(jnp.where(incl, gcb - gct, 0.0)), 0.0)
            eg = jnp.exp(gcb)
            gend = gcb[CHUNK - 1:CHUNK, :] if d == 0 else gcb[0:1, :]
            vc = c_s[pl.ds(r0, CHUNK), h * HD:(h + 1) * HD]
            wq_s[d, h, n, CHUNK:, :] = (qc * eg).astype(BF)
            qkkd_s[d, h, n, 0:CHUNK, :] = (raw[CHUNK:, :CHUNK] * decay[:, :CHUNK]).astype(BF)
            qkkd_s[d, h, n, CHUNK:, :] = jnp.transpose(kc * jnp.exp(gend - gcb)).astype(BF)
            ge_s[d, h, n] = jnp.broadcast_to(jnp.exp(gend), (8, HD))
            a_l.append(jnp.where(strict, bb * raw[:CHUNK] * decay, 0.0))
            x_l.append(jnp.concatenate([vc * bb, kc * (bb * eg)], axis=1))
        def each(fn, *lists):
            return [fn(*vals) for vals in zip(*lists)]

        lhs_of = lambda s: jnp.concatenate([s[0], s[1][:, :CHUNK]], axis=1)
        rhs_of = lambda s: jnp.concatenate([s[0], s[1], s[0]], axis=0)
        mm = lambda sa, sb: jnp.dot(lhs_of(sa), rhs_of(sb), preferred_element_type=F32)
        split_all = lambda ms: [_split(m) for m in ms]

        s_p = split_all([jnp.where(blk16, a, 0.0) for a in a_l])
        y_l = [jnp.where(ii == jj, 1.0, 0.0) - jnp.where(blk16, a, 0.0) for a in a_l]
        s_p = split_all(each(mm, s_p, s_p))
        for level in range(3):
            s_y = split_all(y_l)
            s_next = split_all(each(mm, s_p, s_p)) if level < 2 else None
            y_l = each(lambda y, sy, sp: y + mm(sy, sp), y_l, s_y, s_p)
            s_p = s_next
        mm1 = lambda a, b: jnp.dot(a.astype(BF)[:, :CHUNK], b.astype(BF), preferred_element_type=F32)
        for off_diag in (off32, off64):
            z_l = each(lambda t, a: mm1(t, jnp.where(off_diag, a, 0.0)), y_l, a_l)
            y_l = each(lambda t, z: t - mm1(z, t), y_l, z_l)
        x_l = each(mm, split_all(y_l), split_all(x_l))
        for (h, d, n, r0, *_), x in zip(group, x_l):
            u_s[d, h, pl.ds(r0, CHUNK), :] = x[:, :HD]
            wq_s[d, h, n, 0:CHUNK, :] = x[:, HD:].astype(BF)

    lax.fori_loop(0, n_chunks // DN_CHUNKS, pre_body, 0, unroll=DN_UNROLL)

    for h, d in chains:
        s_s[d, h] = s0_ref[d, h] if has_s0 else jnp.zeros((HD, HD), F32)

    def scan_body(n, carry):
        cs = [n, n_chunks - 1 - n]
        r0s = [pl.multiple_of(c * CHUNK, CHUNK) for c in cs]
        s_l = [s_s[d, h] for h, d in chains]
        ws_l = [jnp.dot(wq_s[d, h, cs[d]], s.astype(BF), preferred_element_type=F32)
                for (h, d), s in zip(chains, s_l)]
        v_l = [u_s[d, h, pl.ds(r0s[d], CHUNK), :] - ws[:CHUNK] for (h, d), ws in zip(chains, ws_l)]
        kv_l = [jnp.dot(qkkd_s[d, h, cs[d]], v.astype(BF), preferred_element_type=F32)
                for (h, d), v in zip(chains, v_l)]
        for (h, d), s, ws, kv in zip(chains, s_l, ws_l, kv_l):
            s_s[d, h] = s * ge_s[d, h, cs[d]][0:1] + kv[CHUNK:]
            out_s = a_s if d == 0 else b_s
            out_s[pl.ds(r0s[d], CHUNK), h * HD:(h + 1) * HD] = ws[CHUNK:] + kv[:CHUNK]
        return carry

    lax.fori_loop(0, n_chunks, scan_body, 0)

    if emit_state:
        for d in range(2):
            for h in range(HG):
                sfin_ref[d, h] = s_s[d, h]

    gdn = gdn_ref[...]
    for h in range(HG):
        hs = slice(h * HD, (h + 1) * HD)
        dg = dg_ref[:, hs]
        o_ref[:, hs] = (_rms(a_s[:, hs] + b_s[:, hs], gdn) * (dg * jax.nn.sigmoid(dg))).astype(BF)


def _dn_call(proj, w_conv, alog_l, dtb_l, gdn, state, l, *, n_batch, L, row_blk0, emit_state, prev_states=None):
    has_s0 = state is not None
    has_prev = prev_states is not None
    n_chunks = L // CHUNK
    W = HG * HD

    def pspec(cb):
        return pl.BlockSpec((L, W), lambda b, j: (row_blk0 + b, cb // HG + j))

    def wspec(off):
        return pl.BlockSpec((None, 3, W), lambda b, j: (l, 0, off // HG + j))

    vec = pl.BlockSpec((1, HD), lambda b, j: (0, 0))
    in_specs = [pspec(CB_DQ), pspec(CB_DK), pspec(CB_DV), pspec(CB_DG),
                pl.BlockSpec((L, LANE), lambda b, j: (row_blk0 + b, CB_GATE)),
                wspec(0), wspec(NH), wspec(2 * NH), vec, vec, vec]
    args = [proj, proj, proj, proj, proj, w_conv, w_conv, w_conv, alog_l, dtb_l, gdn]
    if has_s0:
        in_specs.append(pl.BlockSpec((None, None, 2, HG, HD, HD), lambda b, j: (b, l, 0, j, 0, 0)))
        args.append(state)
    aliases = {}
    if has_prev:
        aliases = {len(args): 1}
        in_specs.append(pl.BlockSpec(memory_space=pl.ANY))
        args.append(prev_states)
    out_specs = [pl.BlockSpec((L, W), lambda b, j: (b, j))]
    out_shape = [jax.ShapeDtypeStruct((n_batch * L, D), BF)]
    if emit_state:
        out_specs.append(pl.BlockSpec((None, None, 2, HG, HD, HD), lambda b, j: (b, l, 0, j, 0, 0)))
        out_shape.append(jax.ShapeDtypeStruct((n_batch, DEPTH, 2, NH, HD, HD), F32))
    return pl.pallas_call(
        functools.partial(_dn_kernel, L=L, has_s0=has_s0, has_prev=has_prev, emit_state=emit_state),
        grid=(n_batch, NH // HG),
        in_specs=in_specs,
        out_specs=out_specs,
        out_shape=out_shape,
        input_output_aliases=aliases,
        scratch_shapes=[
            pltpu.VMEM((L, W), F32), pltpu.VMEM((L, W), F32), pltpu.VMEM((L, W), F32),
            pltpu.VMEM((L, LANE), F32), pltpu.VMEM((L, LANE), F32),
            pltpu.VMEM((2, HG, L, HD), F32),
            pltpu.VMEM((2, HG, n_chunks, 2 * CHUNK, HD), BF),
            pltpu.VMEM((2, HG, n_chunks, CHUNK + HD, CHUNK), BF),
            pltpu.VMEM((2, HG, n_chunks, 8, HD), F32),
            pltpu.VMEM((2, HG, HD, HD), F32),
        ],
        compiler_params=_params("parallel", "parallel"),
        name="deltanet_ctx" if not has_s0 else "deltanet_smp",
    )(*args)


def _sc_kernel(sb_ref, scg_ref, sx_ref, w_ref, o_ref, *, L):
    tn = o_ref.shape[1]
    row = lax.broadcasted_iota(jnp.int32, (L, tn), 0)
    x = scg_ref[...] * sx_ref[...]
    w = w_ref[...]
    xm = jnp.where(row == 0, 0.0, pltpu.roll(x, 1, 0))
    xp = jnp.where(row == L - 1, 0.0, pltpu.roll(x, L - 1, 0))
    y = xm * w[0:1] + x * w[1:2] + xp * w[2:3]
    o_ref[...] = (sb_ref[...] * y).astype(BF)


SC_BLOCK_ELEMS = 256 * 1024


def _sc_call(proj, w_conv_sc, l, *, n_batch, L, row_blk0):
    tn = SC_BLOCK_ELEMS // L

    def pspec(cb):
        return pl.BlockSpec((L, tn), lambda b, j: (row_blk0 + b, cb * LANE // tn + j))

    return pl.pallas_call(
        functools.partial(_sc_kernel, L=L),
        grid=(n_batch, D // tn),
        in_specs=[pspec(CB_SB), pspec(CB_SCG), pspec(CB_SX),
                  pl.BlockSpec((None, 3, tn), lambda b, j: (l, 0, j))],
        out_specs=pl.BlockSpec((L, tn), lambda b, j: (b, j)),
        out_shape=jax.ShapeDtypeStruct((n_batch * L, D), BF),
        compiler_params=_params("parallel", "parallel"),
        name="shortconv",
    )(proj, proj, proj, w_conv_sc)


def _rope(x, cos, sin_signed):
    lane = lax.broadcasted_iota(jnp.int32, x.shape, 1)
    swapped = jnp.where((lane & 63) < 32, pltpu.roll(x, HD - 32, 1), pltpu.roll(x, 32, 1))
    return x * cos + swapped * sin_signed


def _softmax_pv(s_parts, v_parts):
    m = s_parts[0].max(axis=-1, keepdims=True)
    for s in s_parts[1:]:
        m = jnp.maximum(m, s.max(axis=-1, keepdims=True))
    den = None
    acc = None
    for s, v in zip(s_parts, v_parts):
        p = jnp.exp(s - m)
        ps = jnp.sum(p, axis=-1, keepdims=True)
        pv = _mm(p, v)
        den = ps if den is None else den + ps
        acc = pv if acc is None else acc + pv
    return acc / den


def _att_ctx_kernel(q_ref, k_ref, v_ref, gq_ref, gk_ref, *rest):
    o_ref, ko_ref, vo_ref = rest[-3:]
    k = _rms(k_ref[...], gk_ref[...])
    v = v_ref[...]
    ko_ref[...] = k
    vo_ref[...] = v
    gq = gq_ref[...] * (HD ** -0.5)
    kb, vb = k.astype(BF), v.astype(BF)

    def scores(g):
        return [_mm_nt(_rms(q_ref[:, g * HD:(g + 1) * HD], gq), kb)]

    s_next = scores(0)
    for g in range(GQ):
        s_cur = s_next
        if g + 1 < GQ:
            s_next = scores(g + 1)
        o_ref[:, g * HD:(g + 1) * HD] = _softmax_pv(s_cur, [vb]).astype(BF)


def _att_ctx_call(proj, g_q, g_k, l, caches):
    vec = pl.BlockSpec((1, HD), lambda b, g: (0, 0))
    cache_spec = pl.BlockSpec((None, None, SEQ, HD), lambda b, g: (b, l, 0, g))
    cache_shape = jax.ShapeDtypeStruct((BATCH, DEPTH, SEQ, KVH * HD), F32)
    in_specs = [
        pl.BlockSpec((SEQ, GQ * HD), lambda b, g: (b, CB_AQ // GQ + g)),
        pl.BlockSpec((SEQ, HD), lambda b, g: (b, CB_AK + g)),
        pl.BlockSpec((SEQ, HD), lambda b, g: (b, CB_AV + g)),
        vec, vec,
    ]
    args = [proj, proj, proj, g_q, g_k]
    aliases = {}
    if caches is not None:
        aliases = {len(args): 1, len(args) + 1: 2}
        in_specs += [pl.BlockSpec(memory_space=pl.ANY)] * 2
        args += list(caches)
    return pl.pallas_call(
        _att_ctx_kernel,
        grid=(BATCH, KVH),
        in_specs=in_specs,
        out_specs=[pl.BlockSpec((SEQ, GQ * HD), lambda b, g: (b, g)), cache_spec, cache_spec],
        out_shape=[jax.ShapeDtypeStruct((T_CTX, D), BF), cache_shape, cache_shape],
        input_output_aliases=aliases,
        compiler_params=_params("parallel", "parallel"),
        name="attention_ctx",
    )(*args)


QB = 256


def _att_smp_kernel(q_ref, k_ref, v_ref, ck_ref, cv_ref, gq_ref, gk_ref, cosq_ref, sinq_ref, cosk_ref, sink_ref, o_ref):
    k = _rope(_rms(k_ref[...], gk_ref[...]), cosk_ref[...], sink_ref[...])
    gq = gq_ref[...] * (HD ** -0.5)
    cq, sq = cosq_ref[...], sinq_ref[...]
    kb, ckb = k.astype(BF), ck_ref[...].astype(BF)
    vs = [cv_ref[...].astype(BF), v_ref[...].astype(BF)]

    def scores(g):
        q = _rope(_rms(q_ref[:, g * HD:(g + 1) * HD], gq), cq, sq).astype(BF)
        return [_mm_nt(q, ckb), _mm_nt(q, kb)]

    s_next = scores(0)
    for g in range(GQ):
        s_cur = s_next
        if g + 1 < GQ:
            s_next = scores(g + 1)
        o_ref[:, g * HD:(g + 1) * HD] = _softmax_pv(s_cur, vs).astype(BF)


def _att_smp_call(proj, cache_k4, cache_v4, g_q, g_k, cos_t, sin_t, l):
    nq = DEC_SEQ // QB
    smp_blk0 = T_CTX // DEC_SEQ
    q_blk0 = T_CTX // QB
    vec = pl.BlockSpec((1, HD), lambda b, g, i: (0, 0))
    tq = pl.BlockSpec((QB, HD), lambda b, g, i: (i, 0))
    tk = pl.BlockSpec((DEC_SEQ, HD), lambda b, g, i: (0, 0))
    cache = pl.BlockSpec((None, None, PAST, HD), lambda b, g, i: (b, l, 0, g))
    return pl.pallas_call(
        _att_smp_kernel,
        grid=(DEC_BATCH, KVH, nq),
        in_specs=[
            pl.BlockSpec((QB, GQ * HD), lambda b, g, i: (q_blk0 + b * nq + i, CB_AQ // GQ + g)),
            pl.BlockSpec((DEC_SEQ, HD), lambda b, g, i: (smp_blk0 + b, CB_AK + g)),
            pl.BlockSpec((DEC_SEQ, HD), lambda b, g, i: (smp_blk0 + b, CB_AV + g)),
            cache, cache, vec, vec, tq, tq, tk, tk,
        ],
        out_specs=pl.BlockSpec((QB, GQ * HD), lambda b, g, i: (b * nq + i, g)),
        out_shape=jax.ShapeDtypeStruct((T_SMP, D), BF),
        compiler_params=_params("parallel", "parallel", "parallel"),
        name="attention_smp",
    )(proj, proj, proj, cache_k4, cache_v4, g_q, g_k, cos_t, sin_t, cos_t, sin_t)


def _pick(i, ctx_ref, smp_ref):
    return jnp.where(i < N_CTX_TILES, ctx_ref[...], smp_ref[...])


def _mix_kernel(x_ref, ga_ref, gb_ref, gc_ref, ac_ref, as_ref, bc_ref, bs_ref, cc_ref, cs_ref,
                wa_ref, wb_ref, wc_ref, wo_ref, gpost_ref, gt_ref, o_ref):
    i = pl.program_id(0)
    m = jax.nn.sigmoid(ga_ref[...]) * jnp.dot(_pick(i, ac_ref, as_ref), wa_ref[...], preferred_element_type=F32)
    m += jax.nn.sigmoid(gb_ref[...]) * jnp.dot(_pick(i, bc_ref, bs_ref), wb_ref[...], preferred_element_type=F32)
    m += jax.nn.sigmoid(gc_ref[...]) * jnp.dot(_pick(i, cc_ref, cs_ref), wc_ref[...], preferred_element_type=F32)
    mix = jnp.dot(m.astype(BF), wo_ref[...], preferred_element_type=F32)
    o_ref[...] = x_ref[...] + gt_ref[...] * _rms(mix, gpost_ref[...])


def _resident(shape, index_map):
    return pl.BlockSpec(shape, index_map, pipeline_mode=pl.Buffered(1))


def _mix_call(x, proj, branches, weights, gain3, mod4, l):
    ctx_spec = pl.BlockSpec((TM, D), lambda i: (jnp.minimum(i, N_CTX_TILES - 1), 0))
    smp_spec = pl.BlockSpec((TM, D), lambda i: (jnp.maximum(i - N_CTX_TILES, 0), 0))
    wspec = _resident((None, D, D), lambda i: (l, 0, 0))

    def gate_spec(k):
        return pl.BlockSpec((TM, D), lambda i: (i, CB_MG // NH + k))

    in_specs = [pl.BlockSpec((TM, D), lambda i: (i, 0)), gate_spec(0), gate_spec(1), gate_spec(2)]
    args = [x, proj, proj, proj]
    for ctx, smp in branches:
        in_specs += [ctx_spec, smp_spec]
        args += [ctx, smp]
    in_specs += [wspec] * 4 + [_gain_spec(l), _mod_spec(l, 2)]
    args += list(weights) + [gain3, mod4]
    return pl.pallas_call(
        _mix_kernel,
        grid=(N_TILES,),
        in_specs=in_specs,
        out_specs=pl.BlockSpec((TM, D), lambda i: (i, 0)),
        out_shape=jax.ShapeDtypeStruct((T_ALL, D), F32),
        compiler_params=_params("parallel"),
        name="merge_outproj",
    )(*args)


def _mlp_kernel(x_ref, gpre_ref, sh_ref, sc_ref, wu_ref, wd_ref, gpost_ref, gt_ref, *rest, last):
    x = x_ref[...]
    h = _rms(x, gpre_ref[...]) * (1.0 + sc_ref[...]) + sh_ref[...]
    u = jnp.maximum(jnp.dot(h.astype(BF), wu_ref[...], preferred_element_type=F32), 0.0)
    dn = jnp.dot((u * u).astype(BF), wd_ref[...], preferred_element_type=F32)
    y = x + gt_ref[...] * _rms(dn, gpost_ref[...])
    if last:
        yc_ref, ys_ref = rest
        i = pl.program_id(0)

        @pl.when(i < N_CTX_TILES)
        def _():
            yc_ref[...] = y

        @pl.when(i >= N_CTX_TILES)
        def _():
            ys_ref[...] = y
    else:
        gn_ref, shn_ref, scn_ref, y_ref, hn_ref = rest
        y_ref[...] = y
        hn_ref[...] = (_rms(y, gn_ref[...]) * (1.0 + scn_ref[...]) + shn_ref[...]).astype(BF)


def _mlp_call(x, w_up, w_down, gpre3, gpost3, gpre_mix3, mod4, l):
    last = l == DEPTH - 1
    row_spec = pl.BlockSpec((TM, D), lambda i: (i, 0))
    in_specs = [
        row_spec,
        _gain_spec(l), _mod_spec(l, 3), _mod_spec(l, 4),
        _resident((None, D, MLP_H), lambda i: (l, 0, 0)),
        _resident((None, MLP_H, D), lambda i: (l, 0, 0)),
        _gain_spec(l), _mod_spec(l, 5),
    ]
    args = [x, gpre3, mod4, mod4, w_up, w_down, gpost3, mod4]
    if last:
        out_specs = [pl.BlockSpec((TM, D), lambda i: (jnp.minimum(i, N_CTX_TILES - 1), 0)),
                     pl.BlockSpec((TM, D), lambda i: (jnp.maximum(i - N_CTX_TILES, 0), 0))]
        out_shape = [jax.ShapeDtypeStruct((T_CTX, D), F32), jax.ShapeDtypeStruct((T_SMP, D), F32)]
    else:
        in_specs += [_gain_spec(l + 1), _mod_spec(l + 1, 0), _mod_spec(l + 1, 1)]
        args += [gpre_mix3, mod4, mod4]
        out_specs = [row_spec, row_spec]
        out_shape = [jax.ShapeDtypeStruct((T_ALL, D), F32), jax.ShapeDtypeStruct((T_ALL, D), BF)]
    return pl.pallas_call(
        functools.partial(_mlp_kernel, last=last),
        grid=(N_TILES,),
        in_specs=in_specs,
        out_specs=out_specs,
        out_shape=out_shape,
        compiler_params=_params("arbitrary"),
        name="mlp",
    )(*args)


def _rope_tables():
    t = jnp.arange(DEC_SEQ)
    n_freq = HD // 4
    freqs = ROPE_THETA ** (-jnp.arange(n_freq, dtype=F32) / n_freq)
    ang_r = (t // GRID_W).astype(F32)[:, None] * freqs
    ang_c = (t % GRID_W).astype(F32)[:, None] * freqs
    cos_t = jnp.concatenate([jnp.cos(ang_r)] * 2 + [jnp.cos(ang_c)] * 2, axis=1)
    sin_t = jnp.concatenate([-jnp.sin(ang_r), jnp.sin(ang_r), -jnp.sin(ang_c), jnp.sin(ang_c)], axis=1)
    return cos_t, sin_t


def _gate_lanes(p):
    flat = p.reshape(DEPTH, 1, 2 * NH)
    return jnp.pad(flat, ((0, 0), (0, 0), (2 * NH, LANE - 4 * NH)))


def kernel(x_prompt, x_sample, cache_k, cache_v, state_dn, c, c_ctx, w_mod, b_mod, g_pre_mix, g_post_mix, g_pre_mlp, g_post_mlp, w_in, w_conv_dn, a_log, dt_bias, g_dn_out, w_conv_sc, g_q, g_k, w_br_dn, w_br_sc, w_br_att, w_out, w_mlp_up, w_mlp_down):
    x = jnp.concatenate([x_prompt.reshape(T_CTX, D), x_sample.reshape(T_SMP, D)], axis=0)

    cond8 = jnp.zeros((8, D), F32).at[0].set(c_ctx).at[1:1 + DEC_BATCH].set(c)
    mod4 = _mod_call(cond8, w_mod, b_mod).reshape(DEPTH, 8, 1, 6 * D)

    w_a, w_b, w_c, w_o = (w.astype(BF) for w in (w_br_dn, w_br_sc, w_br_att, w_out))
    w_up, w_down = w_mlp_up.astype(BF), w_mlp_down.astype(BF)

    gpm3, gqm3 = g_pre_mix.reshape(DEPTH, 1, D), g_post_mix.reshape(DEPTH, 1, D)
    gpl3, gql3 = g_pre_mlp.reshape(DEPTH, 1, D), g_post_mlp.reshape(DEPTH, 1, D)
    alog_p, dtb_p = _gate_lanes(a_log), _gate_lanes(dt_bias)
    cos_t, sin_t = _rope_tables()
    cache_k4 = cache_k.reshape(DEC_BATCH, DEPTH, PAST, KVH * HD)
    cache_v4 = cache_v.reshape(DEC_BATCH, DEPTH, PAST, KVH * HD)

    w_t = jnp.swapaxes(w_in, 1, 2)

    caches, new_s = None, None
    h = _pre_call(x, gpm3, mod4, 0, 0, 1)
    for l in range(DEPTH):
        proj = _proj_call(h, w_t, l)

        gdn = g_dn_out[l].reshape(1, HD)
        dn_ctx, new_s = _dn_call(proj, w_conv_dn, alog_p[l], dtb_p[l], gdn, None, l,
                                 n_batch=BATCH, L=SEQ, row_blk0=0, emit_state=True, prev_states=new_s)
        (dn_smp,) = _dn_call(proj, w_conv_dn, alog_p[l], dtb_p[l], gdn, state_dn, l,
                             n_batch=DEC_BATCH, L=DEC_SEQ, row_blk0=T_CTX // DEC_SEQ, emit_state=False)
        sc_ctx = _sc_call(proj, w_conv_sc, l, n_batch=BATCH, L=SEQ, row_blk0=0)
        sc_smp = _sc_call(proj, w_conv_sc, l, n_batch=DEC_BATCH, L=DEC_SEQ, row_blk0=T_CTX // DEC_SEQ)
        gq, gk = g_q[l].reshape(1, HD), g_k[l].reshape(1, HD)
        at_ctx, *caches = _att_ctx_call(proj, gq, gk, l, caches)
        at_smp = _att_smp_call(proj, cache_k4, cache_v4, gq, gk, cos_t, sin_t, l)

        x = _mix_call(x, proj, [(dn_ctx, dn_smp), (sc_ctx, sc_smp), (at_ctx, at_smp)],
                      (w_a, w_b, w_c, w_o), gqm3, mod4, l)
        x, h = _mlp_call(x, w_up, w_down, gpl3, gql3, gpm3, mod4, l)

    y_ctx, y_smp = x, h
    new_k, new_v = (t.reshape(BATCH, DEPTH, SEQ, KVH, HD) for t in caches)
    return (y_ctx.reshape(BATCH, SEQ, D), y_smp.reshape(DEC_BATCH, DEC_SEQ, D), new_k, new_v, new_s)
```

```python
import functools

import jax
import jax.numpy as jnp
from jax import lax
from jax.experimental import pallas as pl
from jax.experimental.pallas import tpu as pltpu

D = 1024
BATCH, SEQ = 16, 256
DEC_BATCH, DEC_SEQ = 2, 1024
DEPTH = 4
PAST = 256
GRID_W = 64
NH = 8
HD = 128
KVH = 2
GQ = NH // KVH
CHUNK = 64
MLP_H = 4 * D
EPS = 1e-6
ROPE_THETA = 10000.0

T_CTX = BATCH * SEQ
T_SMP = DEC_BATCH * DEC_SEQ
T_ALL = T_CTX + T_SMP
TM = 512
N_CTX_TILES = T_CTX // TM
N_TILES = T_ALL // TM
TILES_PER_SMP = DEC_SEQ // TM

LANE = 128
CB_DQ, CB_DK, CB_DV, CB_DG, CB_SB, CB_SCG, CB_SX, CB_AQ, CB_MG, CB_AK, CB_AV, CB_GATE = 0, 8, 16, 24, 32, 40, 48, 56, 64, 88, 90, 92
N_CB = 96
PROJ_W = N_CB * LANE
PROJ_TN = 4 * LANE
ROW_CHUNK = 2048

VMEM_LIMIT = 56 * 1024 * 1024

BF = jnp.bfloat16
F32 = jnp.float32
HIGHEST = lax.Precision.HIGHEST


def _params(*sem):
    return pltpu.CompilerParams(dimension_semantics=sem, vmem_limit_bytes=VMEM_LIMIT)


def _mm(a, b):
    return jnp.dot(a.astype(BF), b.astype(BF), preferred_element_type=F32)


def _mm_nt(a, b, precision=None):
    if precision is None:
        a, b = a.astype(BF), b.astype(BF)
    return lax.dot_general(a, b, (((1,), (1,)), ((), ())), precision=precision, preferred_element_type=F32)


def _split(x):
    hi = x.astype(BF)
    return hi, (x - hi.astype(F32)).astype(BF)


def _mm_hi(a, b):
    return jnp.dot(a, b, precision=HIGHEST, preferred_element_type=F32)


def _rms(x, g):
    return x * lax.rsqrt(jnp.mean(x * x, axis=-1, keepdims=True) + EPS) * g


def _cond_row(i):
    return jnp.where(i < N_CTX_TILES, 0, 1 + (i - N_CTX_TILES) // TILES_PER_SMP)


def _mod_kernel(c_ref, w_ref, b_ref, o_ref):
    c = c_ref[...]
    s = c * jax.nn.sigmoid(c)
    o_ref[...] = _mm_hi(s, w_ref[...]) + b_ref[...]


def _mod_call(cond8, w_mod, b_mod):
    tn = 1536
    return pl.pallas_call(
        _mod_kernel,
        grid=(DEPTH, 6 * D // tn),
        in_specs=[
            pl.BlockSpec((8, D), lambda l, j: (0, 0)),
            pl.BlockSpec((None, D, tn), lambda l, j: (l, 0, j)),
            pl.BlockSpec((None, 1, tn), lambda l, j: (l, 0, j)),
        ],
        out_specs=pl.BlockSpec((None, 8, tn), lambda l, j: (l, 0, j)),
        out_shape=jax.ShapeDtypeStruct((DEPTH, 8, 6 * D), F32),
        compiler_params=_params("parallel", "parallel"),
        name="mod",
    )(cond8, w_mod, b_mod.reshape(DEPTH, 1, 6 * D))


def _mod_spec(l, chunk):
    return pl.BlockSpec((None, None, 1, D), lambda i: (l, _cond_row(i), 0, chunk))


def _gain_spec(l):
    return pl.BlockSpec((None, 1, D), lambda i: (l, 0, 0))


def _pre_kernel(x_ref, g_ref, sh_ref, sc_ref, h_ref):
    h = _rms(x_ref[...], g_ref[...]) * (1.0 + sc_ref[...]) + sh_ref[...]
    h_ref[...] = h.astype(BF)


def _pre_call(x, gain3, mod4, l, sh_chunk, sc_chunk):
    return pl.pallas_call(
        _pre_kernel,
        grid=(N_TILES,),
        in_specs=[
            pl.BlockSpec((TM, D), lambda i: (i, 0)),
            _gain_spec(l),
            _mod_spec(l, sh_chunk),
            _mod_spec(l, sc_chunk),
        ],
        out_specs=pl.BlockSpec((TM, D), lambda i: (i, 0)),
        out_shape=jax.ShapeDtypeStruct((T_ALL, D), BF),
        compiler_params=_params("parallel"),
        name="prenorm",
    )(x, gain3, mod4, mod4)


GATE_W = 4 * NH
N_PLAIN = 8
N_SHIFT = 15
J_GATE = N_PLAIN + N_SHIFT
J_AKAV = N_PLAIN + 8
IN_TOTAL = N_PLAIN * PROJ_TN + GATE_W + N_SHIFT * PROJ_TN


def _proj_src(j):
    sub = 8
    return sub * jnp.where(j < N_PLAIN, j * (PROJ_TN // sub),
                           jnp.where(j < J_GATE, j * (PROJ_TN // sub) + GATE_W // sub, N_PLAIN * PROJ_TN // sub))


def _proj_dst(j):
    return jnp.where(j < J_AKAV, j, jnp.where(j == J_AKAV, CB_AK // 4, jnp.where(j < J_GATE, j - 1, J_GATE)))


def _proj_kernel(h_ref, wt_ref, o_ref, w_s):
    j = pl.program_id(0)
    w_s[...] = wt_ref[0].astype(BF)

    def rows(width):
        def body(r, carry):
            r0 = pl.multiple_of(r * ROW_CHUNK, ROW_CHUNK)
            o_ref[pl.ds(r0, ROW_CHUNK), :width] = _mm_nt(h_ref[pl.ds(r0, ROW_CHUNK), :], w_s[:width, :])
            return carry
        lax.fori_loop(0, T_ALL // ROW_CHUNK, body, 0)

    @pl.when(j < J_GATE)
    def _():
        rows(PROJ_TN)

    @pl.when(j == J_GATE)
    def _():
        rows(LANE)

        def zero(r, carry):
            r0 = pl.multiple_of(r * TM, TM)
            o_ref[pl.ds(r0, TM), LANE:] = jnp.zeros((TM, PROJ_TN - LANE), F32)
            return carry
        lax.fori_loop(0, T_ALL // TM, zero, 0)


def _proj_call(h, w_t, l):
    assert w_t.shape == (DEPTH, IN_TOTAL, D)
    return pl.pallas_call(
        _proj_kernel,
        grid=(J_GATE + 1,),
        in_specs=[
            pl.BlockSpec((T_ALL, D), lambda j: (0, 0), pipeline_mode=pl.Buffered(1)),
            pl.BlockSpec((pl.Element(1), pl.Element(PROJ_TN), pl.Element(D)), lambda j: (l, _proj_src(j), 0)),
        ],
        out_specs=pl.BlockSpec((T_ALL, PROJ_TN), lambda j: (0, _proj_dst(j))),
        out_shape=jax.ShapeDtypeStruct((T_ALL, PROJ_W), F32),
        scratch_shapes=[pltpu.VMEM((PROJ_TN, D), BF)],
        compiler_params=_params("arbitrary"),
        name="inproj",
    )(h, w_t)


DN_HEADS = 4
DN_CHAINS = 16
DN_UNROLL = 2


def _dn_kernel(*refs, L, hg, has_s0, has_prev, emit_state):
    refs = list(refs)
    dq_ref, dk_ref, dv_ref, dg_ref, gt_ref, wq_ref, wk_ref, wv_ref, alog_ref, dtb_ref, gdn_ref = refs[:11]
    pos = 11
    s0_ref = None
    if has_s0:
        s0_ref = refs[pos]
        pos += 1
    if has_prev:
        pos += 1
    o_ref = refs[pos]
    pos += 1
    sfin_ref = None
    if emit_state:
        sfin_ref = refs[pos]
        pos += 1
    a_s, b_s, c_s, gc_s, bt_s, u_s, wq_s, qkkd_s, ge_s, s_s = refs[pos:]

    head0 = pl.program_id(1) * hg
    n_chunks = L // CHUNK
    W = hg * HD
    row = lax.broadcasted_iota(jnp.int32, (L, W), 0)

    def conv_silu(x_ref, w_ref):
        x = x_ref[...]
        w = w_ref[...]
        xm = jnp.where(row == 0, 0.0, pltpu.roll(x, 1, 0))
        xp = jnp.where(row == L - 1, 0.0, pltpu.roll(x, L - 1, 0))
        y = xm * w[0:1] + x * w[1:2] + xp * w[2:3]
        return y * jax.nn.sigmoid(y)

    def l2n(x):
        return x * lax.rsqrt(jnp.sum(x * x, axis=-1, keepdims=True) + EPS)

    q = conv_silu(dq_ref, wq_ref)
    k = conv_silu(dk_ref, wk_ref)
    for h in range(hg):
        hs = slice(h * HD, (h + 1) * HD)
        a_s[:, hs] = l2n(q[:, hs]) * (HD ** -0.5)
        b_s[:, hs] = l2n(k[:, hs])
    c_s[...] = conv_silu(dv_ref, wv_ref)

    grow = lax.broadcasted_iota(jnp.int32, (L, LANE), 0)
    glane = lax.broadcasted_iota(jnp.int32, (L, LANE), 1)
    cpos = grow & (CHUNK - 1)
    gates = gt_ref[...]
    bt_s[...] = jax.nn.sigmoid(gates)
    z = gates + dtb_ref[...]
    softplus = jnp.maximum(z, 0.0) + jnp.log(1.0 + jnp.exp(-jnp.abs(z)))
    g_all = -jnp.exp(alog_ref[...]) * softplus
    prefix, suffix = g_all, g_all
    for s in (1, 2, 4, 8, 16, 32):
        prefix = prefix + jnp.where(cpos >= s, pltpu.roll(prefix, s, 0), 0.0)
        suffix = suffix + jnp.where(cpos < CHUNK - s, pltpu.roll(suffix, L - s, 0), 0.0)
    gc_s[...] = jnp.where(glane < 3 * NH, prefix, suffix)

    ii = lax.broadcasted_iota(jnp.int32, (CHUNK, 2 * CHUNK), 0)
    jj = lax.broadcasted_iota(jnp.int32, (CHUNK, 2 * CHUNK), 1) & (CHUNK - 1)
    clane = lax.broadcasted_iota(jnp.int32, (CHUNK, LANE), 1)
    diag = (ii >> 1) == (jj >> 1)
    off_masks = [((ii >> bits) ^ (jj >> bits)) == 1 for bits in range(1, 6)]

    def column(arr, c):
        picked = jnp.sum(jnp.where(clane == c, arr, 0.0), axis=1, keepdims=True)
        return jnp.broadcast_to(picked, (CHUNK, HD))

    chains = [(h, d) for h in range(hg) for d in range(2)]

    chunks_per_iter = DN_CHAINS // (2 * hg)

    def pre_body(m, carry):
        group = []
        for c in range(chunks_per_iter):
            n = m * chunks_per_iter + c
            r0 = pl.multiple_of(n * CHUNK, CHUNK)
            gcr = gc_s[pl.ds(r0, CHUNK), :]
            btr = bt_s[pl.ds(r0, CHUNK), :]
            for h in range(hg):
                qc = a_s[pl.ds(r0, CHUNK), h * HD:(h + 1) * HD]
                kc = b_s[pl.ds(r0, CHUNK), h * HD:(h + 1) * HD]
                raw = _mm_nt(jnp.concatenate([kc, qc], axis=0),
                             jnp.concatenate([kc, kc], axis=0))
                group += [(h, d, n, r0, gcr, btr, qc, kc, raw) for d in range(2)]
        pre_group(group)
        return carry

    def pre_group(group):
        a_l, x_l = [], []
        for h, d, n, r0, gcr, btr, qc, kc, raw in group:
            incl = (ii >= jj) if d == 0 else (ii <= jj)
            strict = (ii > jj) if d == 0 else (ii < jj)
            col = d * NH + head0 + h
            bb = column(btr, col)
            gcb = column(gcr, 2 * NH + col)
            gct = jnp.transpose(jnp.concatenate([gcb, gcb], axis=0))[:CHUNK, :]
            decay = jnp.where(incl, jnp.exp(jnp.where(incl, gcb - gct, 0.0)), 0.0)
            eg = jnp.exp(gcb)
            gend = gcb[CHUNK - 1:CHUNK, :] if d == 0 else gcb[0:1, :]
            vc = c_s[pl.ds(r0, CHUNK), h * HD:(h + 1) * HD]
            wq_s[d, h, n, CHUNK:, :] = (qc * eg).astype(BF)
            qkkd_s[d, h, n, 0:CHUNK, :] = (raw[CHUNK:, :CHUNK] * decay[:, :CHUNK]).astype(BF)
            qkkd_s[d, h, n, CHUNK:, :] = jnp.transpose(kc * jnp.exp(gend - gcb)).astype(BF)
            ge_s[d, h, n] = jnp.broadcast_to(jnp.exp(gend), (8, HD))
            a_l.append(jnp.where(strict, bb * raw[:CHUNK] * decay, 0.0))
            x_l.append(jnp.concatenate([vc * bb, kc * (bb * eg)], axis=1))
        def each(fn, *lists):
            return [fn(*vals) for vals in zip(*lists)]

        mm1 = lambda a, b: jnp.dot(a.astype(BF)[:, :CHUNK], b.astype(BF), preferred_element_type=F32)
        y_l = [jnp.where(ii == jj, 1.0, 0.0) - jnp.where(diag, a, 0.0) for a in a_l]
        for off_diag in off_masks:
            z_l = each(lambda t, a: mm1(t, jnp.where(off_diag, a, 0.0)), y_l, a_l)
            y_l = each(lambda t, z: t - mm1(z, t), y_l, z_l)

        def apply3(t, x):
            th, tl = _split(t)
            xh, xl = _split(x)
            return jnp.dot(jnp.concatenate([th, tl[:, :CHUNK]], axis=1),
                           jnp.concatenate([xh, xl, xh], axis=0), preferred_element_type=F32)
        x_l = each(apply3, y_l, x_l)
        for (h, d, n, r0, *_), x in zip(group, x_l):
            u_s[d, h, pl.ds(r0, CHUNK), :] = x[:, :HD]
            wq_s[d, h, n, 0:CHUNK, :] = x[:, HD:].astype(BF)

    lax.fori_loop(0, n_chunks // chunks_per_iter, pre_body, 0, unroll=DN_UNROLL)

    for h, d in chains:
        s_s[d, h] = s0_ref[d, h] if has_s0 else jnp.zeros((HD, HD), F32)

    def scan_body(n, carry):
        cs = [n, n_chunks - 1 - n]
        r0s = [pl.multiple_of(c * CHUNK, CHUNK) for c in cs]
        s_l = [s_s[d, h] for h, d in chains]
        ws_l = [jnp.dot(wq_s[d, h, cs[d]], s.astype(BF), preferred_element_type=F32)
                for (h, d), s in zip(chains, s_l)]
        v_l = [u_s[d, h, pl.ds(r0s[d], CHUNK), :] - ws[:CHUNK] for (h, d), ws in zip(chains, ws_l)]
        kv_l = [jnp.dot(qkkd_s[d, h, cs[d]], v.astype(BF), preferred_element_type=F32)
                for (h, d), v in zip(chains, v_l)]
        for (h, d), s, ws, kv in zip(chains, s_l, ws_l, kv_l):
            s_s[d, h] = s * ge_s[d, h, cs[d]][0:1] + kv[CHUNK:]
            out_s = a_s if d == 0 else b_s
            out_s[pl.ds(r0s[d], CHUNK), h * HD:(h + 1) * HD] = ws[CHUNK:] + kv[:CHUNK]
        return carry

    lax.fori_loop(0, n_chunks, scan_body, 0, unroll=4)

    if emit_state:
        for d in range(2):
            for h in range(hg):
                sfin_ref[d, h] = s_s[d, h]

    gdn = gdn_ref[...]
    for h in range(hg):
        hs = slice(h * HD, (h + 1) * HD)
        dg = dg_ref[:, hs]
        o_ref[:, hs] = (_rms(a_s[:, hs] + b_s[:, hs], gdn) * (dg * jax.nn.sigmoid(dg))).astype(BF)


def _dn_call(proj, w_conv, alog_l, dtb_l, gdn, state, l, *, n_batch, L, row_blk0, emit_state, prev_states=None):
    has_s0 = state is not None
    has_prev = prev_states is not None
    hg = DN_HEADS
    n_chunks = L // CHUNK
    W = hg * HD

    def pspec(cb):
        return pl.BlockSpec((L, W), lambda b, j: (row_blk0 + b, cb // hg + j))

    def wspec(off):
        return pl.BlockSpec((None, 3, W), lambda b, j: (l, 0, off // hg + j))

    vec = pl.BlockSpec((1, HD), lambda b, j: (0, 0))
    in_specs = [pspec(CB_DQ), pspec(CB_DK), pspec(CB_DV), pspec(CB_DG),
                pl.BlockSpec((L, LANE), lambda b, j: (row_blk0 + b, CB_GATE)),
                wspec(0), wspec(NH), wspec(2 * NH), vec, vec, vec]
    args = [proj, proj, proj, proj, proj, w_conv, w_conv, w_conv, alog_l, dtb_l, gdn]
    if has_s0:
        in_specs.append(pl.BlockSpec((None, None, 2, hg, HD, HD), lambda b, j: (b, l, 0, j, 0, 0)))
        args.append(state)
    aliases = {}
    if has_prev:
        aliases = {len(args): 1}
        in_specs.append(pl.BlockSpec(memory_space=pl.ANY))
        args.append(prev_states)
    out_specs = [pl.BlockSpec((L, W), lambda b, j: (b, j))]
    out_shape = [jax.ShapeDtypeStruct((n_batch * L, D), BF)]
    if emit_state:
        out_specs.append(pl.BlockSpec((None, None, 2, hg, HD, HD), lambda b, j: (b, l, 0, j, 0, 0)))
        out_shape.append(jax.ShapeDtypeStruct((n_batch, DEPTH, 2, NH, HD, HD), F32))
    return pl.pallas_call(
        functools.partial(_dn_kernel, L=L, hg=hg, has_s0=has_s0, has_prev=has_prev, emit_state=emit_state),
        grid=(n_batch, NH // hg),
        in_specs=in_specs,
        out_specs=out_specs,
        out_shape=out_shape,
        input_output_aliases=aliases,
        scratch_shapes=[
            pltpu.VMEM((L, W), F32), pltpu.VMEM((L, W), F32), pltpu.VMEM((L, W), F32),
            pltpu.VMEM((L, LANE), F32), pltpu.VMEM((L, LANE), F32),
            pltpu.VMEM((2, hg, L, HD), F32),
            pltpu.VMEM((2, hg, n_chunks, 2 * CHUNK, HD), BF),
            pltpu.VMEM((2, hg, n_chunks, CHUNK + HD, CHUNK), BF),
            pltpu.VMEM((2, hg, n_chunks, 8, HD), F32),
            pltpu.VMEM((2, hg, HD, HD), F32),
        ],
        compiler_params=_params("parallel", "parallel"),
        name="deltanet_ctx" if not has_s0 else "deltanet_smp",
    )(*args)


def _sc_kernel(sb_ref, scg_ref, sx_ref, w_ref, o_ref, *, L):
    tn = o_ref.shape[1]
    row = lax.broadcasted_iota(jnp.int32, (L, tn), 0)
    x = scg_ref[...] * sx_ref[...]
    w = w_ref[...]
    xm = jnp.where(row == 0, 0.0, pltpu.roll(x, 1, 0))
    xp = jnp.where(row == L - 1, 0.0, pltpu.roll(x, L - 1, 0))
    y = xm * w[0:1] + x * w[1:2] + xp * w[2:3]
    o_ref[...] = (sb_ref[...] * y).astype(BF)


SC_BLOCK_ELEMS = 256 * 1024


def _sc_call(proj, w_conv_sc, l, *, n_batch, L, row_blk0):
    tn = SC_BLOCK_ELEMS // L

    def pspec(cb):
        return pl.BlockSpec((L, tn), lambda b, j: (row_blk0 + b, cb * LANE // tn + j))

    return pl.pallas_call(
        functools.partial(_sc_kernel, L=L),
        grid=(n_batch, D // tn),
        in_specs=[pspec(CB_SB), pspec(CB_SCG), pspec(CB_SX),
                  pl.BlockSpec((None, 3, tn), lambda b, j: (l, 0, j))],
        out_specs=pl.BlockSpec((L, tn), lambda b, j: (b, j)),
        out_shape=jax.ShapeDtypeStruct((n_batch * L, D), BF),
        compiler_params=_params("parallel", "parallel"),
        name="shortconv",
    )(proj, proj, proj, w_conv_sc)


def _rope(x, cos, sin_signed):
    lane = lax.broadcasted_iota(jnp.int32, x.shape, 1)
    swapped = jnp.where((lane & 63) < 32, pltpu.roll(x, HD - 32, 1), pltpu.roll(x, 32, 1))
    return x * cos + swapped * sin_signed


def _softmax_pv(s_parts, v_parts):
    m = s_parts[0].max(axis=-1, keepdims=True)
    for s in s_parts[1:]:
        m = jnp.maximum(m, s.max(axis=-1, keepdims=True))
    den = None
    acc = None
    for s, v in zip(s_parts, v_parts):
        p = jnp.exp(s - m)
        ps = jnp.sum(p, axis=-1, keepdims=True)
        pv = _mm(p, v)
        den = ps if den is None else den + ps
        acc = pv if acc is None else acc + pv
    return acc / den


def _att_ctx_kernel(q_ref, k_ref, v_ref, gq_ref, gk_ref, *rest):
    o_ref, ko_ref, vo_ref = rest[-3:]
    k = _rms(k_ref[...], gk_ref[...])
    v = v_ref[...]
    ko_ref[...] = k
    vo_ref[...] = v
    gq = gq_ref[...] * (HD ** -0.5)
    kb, vb = k.astype(BF), v.astype(BF)

    def scores(g):
        return [_mm_nt(_rms(q_ref[:, g * HD:(g + 1) * HD], gq), kb)]

    s_next = scores(0)
    for g in range(GQ):
        s_cur = s_next
        if g + 1 < GQ:
            s_next = scores(g + 1)
        o_ref[:, g * HD:(g + 1) * HD] = _softmax_pv(s_cur, [vb]).astype(BF)


def _att_ctx_call(proj, g_q, g_k, l, caches):
    vec = pl.BlockSpec((1, HD), lambda b, g: (0, 0))
    cache_spec = pl.BlockSpec((None, None, SEQ, HD), lambda b, g: (b, l, 0, g))
    cache_shape = jax.ShapeDtypeStruct((BATCH, DEPTH, SEQ, KVH * HD), F32)
    in_specs = [
        pl.BlockSpec((SEQ, GQ * HD), lambda b, g: (b, CB_AQ // GQ + g)),
        pl.BlockSpec((SEQ, HD), lambda b, g: (b, CB_AK + g)),
        pl.BlockSpec((SEQ, HD), lambda b, g: (b, CB_AV + g)),
        vec, vec,
    ]
    args = [proj, proj, proj, g_q, g_k]
    aliases = {}
    if caches is not None:
        aliases = {len(args): 1, len(args) + 1: 2}
        in_specs += [pl.BlockSpec(memory_space=pl.ANY)] * 2
        args += list(caches)
    return pl.pallas_call(
        _att_ctx_kernel,
        grid=(BATCH, KVH),
        in_specs=in_specs,
        out_specs=[pl.BlockSpec((SEQ, GQ * HD), lambda b, g: (b, g)), cache_spec, cache_spec],
        out_shape=[jax.ShapeDtypeStruct((T_CTX, D), BF), cache_shape, cache_shape],
        input_output_aliases=aliases,
        compiler_params=_params("parallel", "parallel"),
        name="attention_ctx",
    )(*args)


QB = 256


def _att_smp_kernel(q_ref, k_ref, v_ref, ck_ref, cv_ref, gq_ref, gk_ref, cosq_ref, sinq_ref, cosk_ref, sink_ref, o_ref):
    k = _rope(_rms(k_ref[...], gk_ref[...]), cosk_ref[...], sink_ref[...])
    gq = gq_ref[...] * (HD ** -0.5)
    cq, sq = cosq_ref[...], sinq_ref[...]
    kb, ckb = k.astype(BF), ck_ref[...].astype(BF)
    vs = [cv_ref[...].astype(BF), v_ref[...].astype(BF)]

    def scores(g):
        q = _rope(_rms(q_ref[:, g * HD:(g + 1) * HD], gq), cq, sq).astype(BF)
        return [_mm_nt(q, ckb), _mm_nt(q, kb)]

    s_next = scores(0)
    for g in range(GQ):
        s_cur = s_next
        if g + 1 < GQ:
            s_next = scores(g + 1)
        o_ref[:, g * HD:(g + 1) * HD] = _softmax_pv(s_cur, vs).astype(BF)


def _att_smp_call(proj, cache_k4, cache_v4, g_q, g_k, cos_t, sin_t, l):
    nq = DEC_SEQ // QB
    smp_blk0 = T_CTX // DEC_SEQ
    q_blk0 = T_CTX // QB
    vec = pl.BlockSpec((1, HD), lambda b, g, i: (0, 0))
    tq = pl.BlockSpec((QB, HD), lambda b, g, i: (i, 0))
    tk = pl.BlockSpec((DEC_SEQ, HD), lambda b, g, i: (0, 0))
    cache = pl.BlockSpec((None, None, PAST, HD), lambda b, g, i: (b, l, 0, g))
    return pl.pallas_call(
        _att_smp_kernel,
        grid=(DEC_BATCH, KVH, nq),
        in_specs=[
            pl.BlockSpec((QB, GQ * HD), lambda b, g, i: (q_blk0 + b * nq + i, CB_AQ // GQ + g)),
            pl.BlockSpec((DEC_SEQ, HD), lambda b, g, i: (smp_blk0 + b, CB_AK + g)),
            pl.BlockSpec((DEC_SEQ, HD), lambda b, g, i: (smp_blk0 + b, CB_AV + g)),
            cache, cache, vec, vec, tq, tq, tk, tk,
        ],
        out_specs=pl.BlockSpec((QB, GQ * HD), lambda b, g, i: (b * nq + i, g)),
        out_shape=jax.ShapeDtypeStruct((T_SMP, D), BF),
        compiler_params=_params("parallel", "parallel", "parallel"),
        name="attention_smp",
    )(proj, proj, proj, cache_k4, cache_v4, g_q, g_k, cos_t, sin_t, cos_t, sin_t)


def _pick(i, ctx_ref, smp_ref):
    return jnp.where(i < N_CTX_TILES, ctx_ref[...], smp_ref[...])


def _mix_kernel(x_ref, ga_ref, gb_ref, gc_ref, ac_ref, as_ref, bc_ref, bs_ref, cc_ref, cs_ref,
                wa_ref, wb_ref, wc_ref, wo_ref, gpost_ref, gt_ref, o_ref):
    i = pl.program_id(0)
    m = jax.nn.sigmoid(ga_ref[...]) * jnp.dot(_pick(i, ac_ref, as_ref), wa_ref[...], preferred_element_type=F32)
    m += jax.nn.sigmoid(gb_ref[...]) * jnp.dot(_pick(i, bc_ref, bs_ref), wb_ref[...], preferred_element_type=F32)
    m += jax.nn.sigmoid(gc_ref[...]) * jnp.dot(_pick(i, cc_ref, cs_ref), wc_ref[...], preferred_element_type=F32)
    mix = jnp.dot(m.astype(BF), wo_ref[...], preferred_element_type=F32)
    o_ref[...] = x_ref[...] + gt_ref[...] * _rms(mix, gpost_ref[...])


def _resident(shape, index_map):
    return pl.BlockSpec(shape, index_map, pipeline_mode=pl.Buffered(1))


def _mix_call(x, proj, branches, weights, gain3, mod4, l):
    ctx_spec = pl.BlockSpec((TM, D), lambda i: (jnp.minimum(i, N_CTX_TILES - 1), 0))
    smp_spec = pl.BlockSpec((TM, D), lambda i: (jnp.maximum(i - N_CTX_TILES, 0), 0))
    wspec = _resident((None, D, D), lambda i: (l, 0, 0))

    def gate_spec(k):
        return pl.BlockSpec((TM, D), lambda i: (i, CB_MG // NH + k))

    in_specs = [pl.BlockSpec((TM, D), lambda i: (i, 0)), gate_spec(0), gate_spec(1), gate_spec(2)]
    args = [x, proj, proj, proj]
    for ctx, smp in branches:
        in_specs += [ctx_spec, smp_spec]
        args += [ctx, smp]
    in_specs += [wspec] * 4 + [_gain_spec(l), _mod_spec(l, 2)]
    args += list(weights) + [gain3, mod4]
    return pl.pallas_call(
        _mix_kernel,
        grid=(N_TILES,),
        in_specs=in_specs,
        out_specs=pl.BlockSpec((TM, D), lambda i: (i, 0)),
        out_shape=jax.ShapeDtypeStruct((T_ALL, D), F32),
        compiler_params=_params("parallel"),
        name="merge_outproj",
    )(*args)


def _mlp_kernel(x_ref, gpre_ref, sh_ref, sc_ref, wu_ref, wd_ref, gpost_ref, gt_ref, *rest, last):
    x = x_ref[...]
    h = _rms(x, gpre_ref[...]) * (1.0 + sc_ref[...]) + sh_ref[...]
    u = jnp.maximum(jnp.dot(h.astype(BF), wu_ref[...], preferred_element_type=F32), 0.0)
    dn = jnp.dot((u * u).astype(BF), wd_ref[...], preferred_element_type=F32)
    y = x + gt_ref[...] * _rms(dn, gpost_ref[...])
    if last:
        yc_ref, ys_ref = rest
        i = pl.program_id(0)

        @pl.when(i < N_CTX_TILES)
        def _():
            yc_ref[...] = y

        @pl.when(i >= N_CTX_TILES)
        def _():
            ys_ref[...] = y
    else:
        gn_ref, shn_ref, scn_ref, y_ref, hn_ref = rest
        y_ref[...] = y
        hn_ref[...] = (_rms(y, gn_ref[...]) * (1.0 + scn_ref[...]) + shn_ref[...]).astype(BF)


def _mlp_call(x, w_up, w_down, gpre3, gpost3, gpre_mix3, mod4, l):
    last = l == DEPTH - 1
    row_spec = pl.BlockSpec((TM, D), lambda i: (i, 0))
    in_specs = [
        row_spec,
        _gain_spec(l), _mod_spec(l, 3), _mod_spec(l, 4),
        _resident((None, D, MLP_H), lambda i: (l, 0, 0)),
        _resident((None, MLP_H, D), lambda i: (l, 0, 0)),
        _gain_spec(l), _mod_spec(l, 5),
    ]
    args = [x, gpre3, mod4, mod4, w_up, w_down, gpost3, mod4]
    if last:
        out_specs = [pl.BlockSpec((TM, D), lambda i: (jnp.minimum(i, N_CTX_TILES - 1), 0)),
                     pl.BlockSpec((TM, D), lambda i: (jnp.maximum(i - N_CTX_TILES, 0), 0))]
        out_shape = [jax.ShapeDtypeStruct((T_CTX, D), F32), jax.ShapeDtypeStruct((T_SMP, D), F32)]
    else:
        in_specs += [_gain_spec(l + 1), _mod_spec(l + 1, 0), _mod_spec(l + 1, 1)]
        args += [gpre_mix3, mod4, mod4]
        out_specs = [row_spec, row_spec]
        out_shape = [jax.ShapeDtypeStruct((T_ALL, D), F32), jax.ShapeDtypeStruct((T_ALL, D), BF)]
    return pl.pallas_call(
        functools.partial(_mlp_kernel, last=last),
        grid=(N_TILES,),
        in_specs=in_specs,
        out_specs=out_specs,
        out_shape=out_shape,
        compiler_params=_params("arbitrary"),
        name="mlp",
    )(*args)


def _rope_tables():
    t = jnp.arange(DEC_SEQ)
    n_freq = HD // 4
    freqs = ROPE_THETA ** (-jnp.arange(n_freq, dtype=F32) / n_freq)
    ang_r = (t // GRID_W).astype(F32)[:, None] * freqs
    ang_c = (t % GRID_W).astype(F32)[:, None] * freqs
    cos_t = jnp.concatenate([jnp.cos(ang_r)] * 2 + [jnp.cos(ang_c)] * 2, axis=1)
    sin_t = jnp.concatenate([-jnp.sin(ang_r), jnp.sin(ang_r), -jnp.sin(ang_c), jnp.sin(ang_c)], axis=1)
    return cos_t, sin_t


def _gate_lanes(p):
    flat = p.reshape(DEPTH, 1, 2 * NH)
    return jnp.pad(flat, ((0, 0), (0, 0), (2 * NH, LANE - 4 * NH)))


def kernel(x_prompt, x_sample, cache_k, cache_v, state_dn, c, c_ctx, w_mod, b_mod, g_pre_mix, g_post_mix, g_pre_mlp, g_post_mlp, w_in, w_conv_dn, a_log, dt_bias, g_dn_out, w_conv_sc, g_q, g_k, w_br_dn, w_br_sc, w_br_att, w_out, w_mlp_up, w_mlp_down):
    x = jnp.concatenate([x_prompt.reshape(T_CTX, D), x_sample.reshape(T_SMP, D)], axis=0)

    cond8 = jnp.zeros((8, D), F32).at[0].set(c_ctx).at[1:1 + DEC_BATCH].set(c)
    mod4 = _mod_call(cond8, w_mod, b_mod).reshape(DEPTH, 8, 1, 6 * D)

    w_a, w_b, w_c, w_o = (w.astype(BF) for w in (w_br_dn, w_br_sc, w_br_att, w_out))
    w_up, w_down = w_mlp_up.astype(BF), w_mlp_down.astype(BF)

    gpm3, gqm3 = g_pre_mix.reshape(DEPTH, 1, D), g_post_mix.reshape(DEPTH, 1, D)
    gpl3, gql3 = g_pre_mlp.reshape(DEPTH, 1, D), g_post_mlp.reshape(DEPTH, 1, D)
    alog_p, dtb_p = _gate_lanes(a_log), _gate_lanes(dt_bias)
    cos_t, sin_t = _rope_tables()
    cache_k4 = cache_k.reshape(DEC_BATCH, DEPTH, PAST, KVH * HD)
    cache_v4 = cache_v.reshape(DEC_BATCH, DEPTH, PAST, KVH * HD)

    w_t = jnp.swapaxes(w_in, 1, 2)

    caches, new_s = None, None
    h = _pre_call(x, gpm3, mod4, 0, 0, 1)
    for l in range(DEPTH):
        proj = _proj_call(h, w_t, l)

        gdn = g_dn_out[l].reshape(1, HD)
        dn_ctx, new_s = _dn_call(proj, w_conv_dn, alog_p[l], dtb_p[l], gdn, None, l,
                                 n_batch=BATCH, L=SEQ, row_blk0=0, emit_state=True, prev_states=new_s)
        (dn_smp,) = _dn_call(proj, w_conv_dn, alog_p[l], dtb_p[l], gdn, state_dn, l,
                             n_batch=DEC_BATCH, L=DEC_SEQ, row_blk0=T_CTX // DEC_SEQ, emit_state=False)
        sc_ctx = _sc_call(proj, w_conv_sc, l, n_batch=BATCH, L=SEQ, row_blk0=0)
        sc_smp = _sc_call(proj, w_conv_sc, l, n_batch=DEC_BATCH, L=DEC_SEQ, row_blk0=T_CTX // DEC_SEQ)
        gq, gk = g_q[l].reshape(1, HD), g_k[l].reshape(1, HD)
        at_ctx, *caches = _att_ctx_call(proj, gq, gk, l, caches)
        at_smp = _att_smp_call(proj, cache_k4, cache_v4, gq, gk, cos_t, sin_t, l)

        x = _mix_call(x, proj, [(dn_ctx, dn_smp), (sc_ctx, sc_smp), (at_ctx, at_smp)],
                      (w_a, w_b, w_c, w_o), gqm3, mod4, l)
        x, h = _mlp_call(x, w_up, w_down, gpl3, gql3, gpm3, mod4, l)

    y_ctx, y_smp = x, h
    new_k, new_v = (t.reshape(BATCH, DEPTH, SEQ, KVH, HD) for t in caches)
    return (y_ctx.reshape(BATCH, SEQ, D), y_smp.reshape(DEC_BATCH, DEC_SEQ, D), new_k, new_v, new_s)
```

```python
import functools

import jax
import jax.numpy as jnp
from jax import lax
from jax.experimental import pallas as pl
from jax.experimental.pallas import tpu as pltpu

D = 1024
BATCH, SEQ = 16, 256
DEC_BATCH, DEC_SEQ = 2, 1024
DEPTH = 4
PAST = 256
GRID_W = 64
NH = 8
HD = 128
KVH = 2
GQ = NH // KVH
CHUNK = 64
MLP_H = 4 * D
EPS = 1e-6
ROPE_THETA = 10000.0

T_CTX = BATCH * SEQ
T_SMP = DEC_BATCH * DEC_SEQ
T_ALL = T_CTX + T_SMP
TM = 512
N_CTX_TILES = T_CTX // TM
N_TILES = T_ALL // TM
TILES_PER_SMP = DEC_SEQ // TM

LANE = 128
CB_DQ, CB_DK, CB_DV, CB_DG, CB_SB, CB_SCG, CB_SX, CB_AQ, CB_MG, CB_AK, CB_AV, CB_GATE = 0, 8, 16, 24, 32, 40, 48, 56, 64, 88, 90, 92
N_CB = 96
PROJ_W = N_CB * LANE
PROJ_TN = 4 * LANE
ROW_CHUNK = 2048

VMEM_LIMIT = 56 * 1024 * 1024

BF = jnp.bfloat16
F32 = jnp.float32
HIGHEST = lax.Precision.HIGHEST


def _params(*sem):
    return pltpu.CompilerParams(dimension_semantics=sem, vmem_limit_bytes=VMEM_LIMIT)


def _mm(a, b):
    return jnp.dot(a.astype(BF), b.astype(BF), preferred_element_type=F32)


def _mm_nt(a, b, precision=None):
    if precision is None:
        a, b = a.astype(BF), b.astype(BF)
    return lax.dot_general(a, b, (((1,), (1,)), ((), ())), precision=precision, preferred_element_type=F32)


def _split(x):
    hi = x.astype(BF)
    return hi, (x - hi.astype(F32)).astype(BF)


def _mm_hi(a, b):
    return jnp.dot(a, b, precision=HIGHEST, preferred_element_type=F32)


def _rms(x, g):
    return x * lax.rsqrt(jnp.mean(x * x, axis=-1, keepdims=True) + EPS) * g


def _cond_row(i):
    return jnp.where(i < N_CTX_TILES, 0, 1 + (i - N_CTX_TILES) // TILES_PER_SMP)


def _mod_kernel(c_ref, w_ref, b_ref, o_ref):
    c = c_ref[...]
    s = c * jax.nn.sigmoid(c)
    o_ref[...] = _mm_hi(s, w_ref[...]) + b_ref[...]


def _mod_call(cond8, w_mod, b_mod):
    tn = 1536
    return pl.pallas_call(
        _mod_kernel,
        grid=(DEPTH, 6 * D // tn),
        in_specs=[
            pl.BlockSpec((8, D), lambda l, j: (0, 0)),
            pl.BlockSpec((None, D, tn), lambda l, j: (l, 0, j)),
            pl.BlockSpec((None, 1, tn), lambda l, j: (l, 0, j)),
        ],
        out_specs=pl.BlockSpec((None, 8, tn), lambda l, j: (l, 0, j)),
        out_shape=jax.ShapeDtypeStruct((DEPTH, 8, 6 * D), F32),
        compiler_params=_params("parallel", "parallel"),
        name="mod",
    )(cond8, w_mod, b_mod.reshape(DEPTH, 1, 6 * D))


def _mod_spec(l, chunk):
    return pl.BlockSpec((None, None, 1, D), lambda i: (l, _cond_row(i), 0, chunk))


def _gain_spec(l):
    return pl.BlockSpec((None, 1, D), lambda i: (l, 0, 0))


def _pre_kernel(x_ref, g_ref, sh_ref, sc_ref, h_ref):
    h = _rms(x_ref[...], g_ref[...]) * (1.0 + sc_ref[...]) + sh_ref[...]
    h_ref[...] = h.astype(BF)


def _pre_call(x, gain3, mod4, l, sh_chunk, sc_chunk):
    return pl.pallas_call(
        _pre_kernel,
        grid=(N_TILES,),
        in_specs=[
            pl.BlockSpec((TM, D), lambda i: (i, 0)),
            _gain_spec(l),
            _mod_spec(l, sh_chunk),
            _mod_spec(l, sc_chunk),
        ],
        out_specs=pl.BlockSpec((TM, D), lambda i: (i, 0)),
        out_shape=jax.ShapeDtypeStruct((T_ALL, D), BF),
        compiler_params=_params("parallel"),
        name="prenorm",
    )(x, gain3, mod4, mod4)


GATE_W = 4 * NH
N_PLAIN = 8
N_SHIFT = 15
J_GATE = N_PLAIN + N_SHIFT
J_AKAV = N_PLAIN + 8
IN_TOTAL = N_PLAIN * PROJ_TN + GATE_W + N_SHIFT * PROJ_TN


def _proj_src(j):
    sub = 8
    return sub * jnp.where(j < N_PLAIN, j * (PROJ_TN // sub),
                           jnp.where(j < J_GATE, j * (PROJ_TN // sub) + GATE_W // sub, N_PLAIN * PROJ_TN // sub))


def _proj_dst(j):
    return jnp.where(j < J_AKAV, j, jnp.where(j == J_AKAV, CB_AK // 4, jnp.where(j < J_GATE, j - 1, J_GATE)))


def _proj_kernel(h_ref, wt_ref, o_ref, w_s):
    j = pl.program_id(0)
    w_s[...] = wt_ref[0].astype(BF)

    def rows(width):
        def body(r, carry):
            r0 = pl.multiple_of(r * ROW_CHUNK, ROW_CHUNK)
            o_ref[pl.ds(r0, ROW_CHUNK), :width] = _mm_nt(h_ref[pl.ds(r0, ROW_CHUNK), :], w_s[:width, :])
            return carry
        lax.fori_loop(0, T_ALL // ROW_CHUNK, body, 0)

    @pl.when(j < J_GATE)
    def _():
        rows(PROJ_TN)

    @pl.when(j == J_GATE)
    def _():
        rows(LANE)

        def zero(r, carry):
            r0 = pl.multiple_of(r * TM, TM)
            o_ref[pl.ds(r0, TM), LANE:] = jnp.zeros((TM, PROJ_TN - LANE), F32)
            return carry
        lax.fori_loop(0, T_ALL // TM, zero, 0)


def _proj_call(h, w_t, l):
    assert w_t.shape == (DEPTH, IN_TOTAL, D)
    return pl.pallas_call(
        _proj_kernel,
        grid=(J_GATE + 1,),
        in_specs=[
            pl.BlockSpec((T_ALL, D), lambda j: (0, 0), pipeline_mode=pl.Buffered(1)),
            pl.BlockSpec((pl.Element(1), pl.Element(PROJ_TN), pl.Element(D)), lambda j: (l, _proj_src(j), 0)),
        ],
        out_specs=pl.BlockSpec((T_ALL, PROJ_TN), lambda j: (0, _proj_dst(j))),
        out_shape=jax.ShapeDtypeStruct((T_ALL, PROJ_W), F32),
        scratch_shapes=[pltpu.VMEM((PROJ_TN, D), BF)],
        compiler_params=_params("arbitrary"),
        name="inproj",
    )(h, w_t)


DN_HEADS = 4
DN_CHAINS = 16
DN_UNROLL = 2


def _dn_kernel(*refs, L, hg, has_s0, has_prev, emit_state):
    refs = list(refs)
    dq_ref, dk_ref, dv_ref, dg_ref, gt_ref, wq_ref, wk_ref, wv_ref, alog_ref, dtb_ref, gdn_ref = refs[:11]
    pos = 11
    s0_ref = None
    if has_s0:
        s0_ref = refs[pos]
        pos += 1
    if has_prev:
        pos += 1
    o_ref = refs[pos]
    pos += 1
    sfin_ref = None
    if emit_state:
        sfin_ref = refs[pos]
        pos += 1
    a_s, b_s, c_s, gc_s, bt_s, u_s, wq_s, qkkd_s, ge_s, s_s = refs[pos:]

    head0 = pl.program_id(1) * hg
    n_chunks = L // CHUNK
    W = hg * HD
    row = lax.broadcasted_iota(jnp.int32, (L, W), 0)

    def conv_silu(x_ref, w_ref):
        x = x_ref[...]
        w = w_ref[...]
        xm = jnp.where(row == 0, 0.0, pltpu.roll(x, 1, 0))
        xp = jnp.where(row == L - 1, 0.0, pltpu.roll(x, L - 1, 0))
        y = xm * w[0:1] + x * w[1:2] + xp * w[2:3]
        return y * jax.nn.sigmoid(y)

    def l2n(x):
        return x * lax.rsqrt(jnp.sum(x * x, axis=-1, keepdims=True) + EPS)

    q = conv_silu(dq_ref, wq_ref)
    k = conv_silu(dk_ref, wk_ref)
    for h in range(hg):
        hs = slice(h * HD, (h + 1) * HD)
        a_s[:, hs] = l2n(q[:, hs]) * (HD ** -0.5)
        b_s[:, hs] = l2n(k[:, hs])
    c_s[...] = conv_silu(dv_ref, wv_ref)

    grow = lax.broadcasted_iota(jnp.int32, (L, LANE), 0)
    glane = lax.broadcasted_iota(jnp.int32, (L, LANE), 1)
    cpos = grow & (CHUNK - 1)
    gates = gt_ref[...]
    bt_s[...] = jax.nn.sigmoid(gates)
    z = gates + dtb_ref[...]
    softplus = jnp.maximum(z, 0.0) + jnp.log(1.0 + jnp.exp(-jnp.abs(z)))
    g_all = -jnp.exp(alog_ref[...]) * softplus
    prefix, suffix = g_all, g_all
    for s in (1, 2, 4, 8, 16, 32):
        prefix = prefix + jnp.where(cpos >= s, pltpu.roll(prefix, s, 0), 0.0)
        suffix = suffix + jnp.where(cpos < CHUNK - s, pltpu.roll(suffix, L - s, 0), 0.0)
    gc_s[...] = jnp.where(glane < 3 * NH, prefix, suffix)

    ii = lax.broadcasted_iota(jnp.int32, (CHUNK, 2 * CHUNK), 0)
    jj = lax.broadcasted_iota(jnp.int32, (CHUNK, 2 * CHUNK), 1) & (CHUNK - 1)
    clane = lax.broadcasted_iota(jnp.int32, (CHUNK, LANE), 1)
    diag = (ii >> 1) == (jj >> 1)
    off_masks = [((ii >> bits) ^ (jj >> bits)) == 1 for bits in range(1, 6)]

    def column(arr, c):
        picked = jnp.sum(jnp.where(clane == c, arr, 0.0), axis=1, keepdims=True)
        return jnp.broadcast_to(picked, (CHUNK, HD))

    chains = [(h, d) for h in range(hg) for d in range(2)]

    chunks_per_iter = DN_CHAINS // (2 * hg)

    def pre_body(m, carry):
        group = []
        for c in range(chunks_per_iter):
            n = m * chunks_per_iter + c
            r0 = pl.multiple_of(n * CHUNK, CHUNK)
            gcr = gc_s[pl.ds(r0, CHUNK), :]
            btr = bt_s[pl.ds(r0, CHUNK), :]
            for h in range(hg):
                qc = a_s[pl.ds(r0, CHUNK), h * HD:(h + 1) * HD]
                kc = b_s[pl.ds(r0, CHUNK), h * HD:(h + 1) * HD]
                raw = _mm_nt(jnp.concatenate([kc, qc], axis=0),
                             jnp.concatenate([kc, kc], axis=0))
                group += [(h, d, n, r0, gcr, btr, qc, kc, raw) for d in range(2)]
        pre_group(group)
        return carry

    def pre_group(group):
        a_l, x_l = [], []
        for h, d, n, r0, gcr, btr, qc, kc, raw in group:
            incl = (ii >= jj) if d == 0 else (ii <= jj)
            strict = (ii > jj) if d == 0 else (ii < jj)
            col = d * NH + head0 + h
            bb = column(btr, col)
            gcb = column(gcr, 2 * NH + col)
            gct = jnp.transpose(jnp.concatenate([gcb, gcb], axis=0))[:CHUNK, :]
            decay = jnp.where(incl, jnp.exp(jnp.where(incl, gcb - gct, 0.0)), 0.0)
            eg = jnp.exp(gcb)
            gend = gcb[CHUNK - 1:CHUNK, :] if d == 0 else gcb[0:1, :]
            vc = c_s[pl.ds(r0, CHUNK), h * HD:(h + 1) * HD]
            wq_s[d, h, n, CHUNK:, :] = (qc * eg).astype(BF)
            qkkd_s[d, h, n, 0:CHUNK, :] = (raw[CHUNK:, :CHUNK] * decay[:, :CHUNK]).astype(BF)
            qkkd_s[d, h, n, CHUNK:, :] = jnp.transpose(kc * jnp.exp(gend - gcb)).astype(BF)
            ge_s[d, h, n] = jnp.broadcast_to(jnp.exp(gend), (8, HD))
            a_l.append(jnp.where(strict, bb * raw[:CHUNK] * decay, 0.0))
            x_l.append(jnp.concatenate([vc * bb, kc * (bb * eg)], axis=1))
        def each(fn, *lists):
            return [fn(*vals) for vals in zip(*lists)]

        mm1 = lambda a, b: jnp.dot(a.astype(BF)[:, :CHUNK], b.astype(BF), preferred_element_type=F32)
        y_l = [jnp.where(ii == jj, 1.0, 0.0) - jnp.where(diag, a, 0.0) for a in a_l]
        for off_diag in off_masks:
            z_l = each(lambda t, a: mm1(t, jnp.where(off_diag, a, 0.0)), y_l, a_l)
            y_l = each(lambda t, z: t - mm1(z, t), y_l, z_l)

        def apply3(t, x):
            th, tl = _split(t)
            xh, xl = _split(x)
            return jnp.dot(jnp.concatenate([th, tl[:, :CHUNK]], axis=1),
                           jnp.concatenate([xh, xl, xh], axis=0), preferred_element_type=F32)
        x_l = each(apply3, y_l, x_l)
        for (h, d, n, r0, *_), x in zip(group, x_l):
            u_s[d, h, pl.ds(r0, CHUNK), :] = x[:, :HD]
            wq_s[d, h, n, 0:CHUNK, :] = x[:, HD:].astype(BF)

    lax.fori_loop(0, n_chunks // chunks_per_iter, pre_body, 0, unroll=DN_UNROLL)

    for h, d in chains:
        s_s[d, h] = s0_ref[d, h] if has_s0 else jnp.zeros((HD, HD), F32)

    def scan_body(n, carry):
        cs = [n, n_chunks - 1 - n]
        r0s = [pl.multiple_of(c * CHUNK, CHUNK) for c in cs]
        s_l = [s_s[d, h] for h, d in chains]
        ws_l = [jnp.dot(wq_s[d, h, cs[d]], s.astype(BF), preferred_element_type=F32)
                for (h, d), s in zip(chains, s_l)]
        v_l = [u_s[d, h, pl.ds(r0s[d], CHUNK), :] - ws[:CHUNK] for (h, d), ws in zip(chains, ws_l)]
        kv_l = [jnp.dot(qkkd_s[d, h, cs[d]], v.astype(BF), preferred_element_type=F32)
                for (h, d), v in zip(chains, v_l)]
        for (h, d), s, ws, kv in zip(chains, s_l, ws_l, kv_l):
            s_s[d, h] = s * ge_s[d, h, cs[d]][0:1] + kv[CHUNK:]
            out_s = a_s if d == 0 else b_s
            out_s[pl.ds(r0s[d], CHUNK), h * HD:(h + 1) * HD] = ws[CHUNK:] + kv[:CHUNK]
        return carry

    lax.fori_loop(0, n_chunks, scan_body, 0, unroll=4)

    if emit_state:
        for d in range(2):
            for h in range(hg):
                sfin_ref[d, h] = s_s[d, h]

    gdn = gdn_ref[...]
    for h in range(hg):
        hs = slice(h * HD, (h + 1) * HD)
        dg = dg_ref[:, hs]
        o_ref[:, hs] = (_rms(a_s[:, hs] + b_s[:, hs], gdn) * (dg * jax.nn.sigmoid(dg))).astype(BF)


def _dn_call(proj, w_conv, alog_l, dtb_l, gdn, state, l, *, n_batch, L, row_blk0, emit_state, prev_states=None):
    has_s0 = state is not None
    has_prev = prev_states is not None
    hg = DN_HEADS
    n_chunks = L // CHUNK
    W = hg * HD

    def pspec(cb):
        return pl.BlockSpec((L, W), lambda b, j: (row_blk0 + b, cb // hg + j))

    def wspec(off):
        return pl.BlockSpec((None, 3, W), lambda b, j: (l, 0, off // hg + j))

    vec = pl.BlockSpec((1, HD), lambda b, j: (0, 0))
    in_specs = [pspec(CB_DQ), pspec(CB_DK), pspec(CB_DV), pspec(CB_DG),
                pl.BlockSpec((L, LANE), lambda b, j: (row_blk0 + b, CB_GATE)),
                wspec(0), wspec(NH), wspec(2 * NH), vec, vec, vec]
    args = [proj, proj, proj, proj, proj, w_conv, w_conv, w_conv, alog_l, dtb_l, gdn]
    if has_s0:
        in_specs.append(pl.BlockSpec((None, None, 2, hg, HD, HD), lambda b, j: (b, l, 0, j, 0, 0)))
        args.append(state)
    aliases = {}
    if has_prev:
        aliases = {len(args): 1}
        in_specs.append(pl.BlockSpec(memory_space=pl.ANY))
        args.append(prev_states)
    out_specs = [pl.BlockSpec((L, W), lambda b, j: (b, j))]
    out_shape = [jax.ShapeDtypeStruct((n_batch * L, D), BF)]
    if emit_state:
        out_specs.append(pl.BlockSpec((None, None, 2, hg, HD, HD), lambda b, j: (b, l, 0, j, 0, 0)))
        out_shape.append(jax.ShapeDtypeStruct((n_batch, DEPTH, 2, NH, HD, HD), F32))
    return pl.pallas_call(
        functools.partial(_dn_kernel, L=L, hg=hg, has_s0=has_s0, has_prev=has_prev, emit_state=emit_state),
        grid=(n_batch, NH // hg),
        in_specs=in_specs,
        out_specs=out_specs,
        out_shape=out_shape,
        input_output_aliases=aliases,
        scratch_shapes=[
            pltpu.VMEM((L, W), F32), pltpu.VMEM((L, W), F32), pltpu.VMEM((L, W), F32),
            pltpu.VMEM((L, LANE), F32), pltpu.VMEM((L, LANE), F32),
            pltpu.VMEM((2, hg, L, HD), F32),
            pltpu.VMEM((2, hg, n_chunks, 2 * CHUNK, HD), BF),
            pltpu.VMEM((2, hg, n_chunks, CHUNK + HD, CHUNK), BF),
            pltpu.VMEM((2, hg, n_chunks, 8, HD), F32),
            pltpu.VMEM((2, hg, HD, HD), F32),
        ],
        compiler_params=_params("parallel", "parallel"),
        name="deltanet_ctx" if not has_s0 else "deltanet_smp",
    )(*args)


def _sc_kernel(sb_ref, scg_ref, sx_ref, w_ref, o_ref, *, L):
    tn = o_ref.shape[1]
    row = lax.broadcasted_iota(jnp.int32, (L, tn), 0)
    x = scg_ref[...] * sx_ref[...]
    w = w_ref[...]
    xm = jnp.where(row == 0, 0.0, pltpu.roll(x, 1, 0))
    xp = jnp.where(row == L - 1, 0.0, pltpu.roll(x, L - 1, 0))
    y = xm * w[0:1] + x * w[1:2] + xp * w[2:3]
    o_ref[...] = (sb_ref[...] * y).astype(BF)


SC_BLOCK_ELEMS = 256 * 1024


def _sc_call(proj, w_conv_sc, l, *, n_batch, L, row_blk0):
    tn = SC_BLOCK_ELEMS // L

    def pspec(cb):
        return pl.BlockSpec((L, tn), lambda b, j: (row_blk0 + b, cb * LANE // tn + j))

    return pl.pallas_call(
        functools.partial(_sc_kernel, L=L),
        grid=(n_batch, D // tn),
        in_specs=[pspec(CB_SB), pspec(CB_SCG), pspec(CB_SX),
                  pl.BlockSpec((None, 3, tn), lambda b, j: (l, 0, j))],
        out_specs=pl.BlockSpec((L, tn), lambda b, j: (b, j)),
        out_shape=jax.ShapeDtypeStruct((n_batch * L, D), BF),
        compiler_params=_params("parallel", "parallel"),
        name="shortconv",
    )(proj, proj, proj, w_conv_sc)


def _rope(x, cos, sin_signed):
    lane = lax.broadcasted_iota(jnp.int32, x.shape, 1)
    swapped = jnp.where((lane & 63) < 32, pltpu.roll(x, HD - 32, 1), pltpu.roll(x, 32, 1))
    return x * cos + swapped * sin_signed


def _softmax_pv(s_parts, v_parts):
    m = s_parts[0].max(axis=-1, keepdims=True)
    for s in s_parts[1:]:
        m = jnp.maximum(m, s.max(axis=-1, keepdims=True))
    den = None
    acc = None
    for s, v in zip(s_parts, v_parts):
        p = jnp.exp(s - m)
        ps = jnp.sum(p, axis=-1, keepdims=True)
        pv = _mm(p, v)
        den = ps if den is None else den + ps
        acc = pv if acc is None else acc + pv
    return acc / den


def _pipelined(items, scores, finish):
    s_next = scores(items[0])
    for n, item in enumerate(items):
        s_cur = s_next
        if n + 1 < len(items):
            s_next = scores(items[n + 1])
        finish(item, s_cur)


def _att_ctx_kernel(q_ref, k_ref, v_ref, gq_ref, gk_ref, *rest):
    o_ref, ko_ref, vo_ref = rest[-3:]
    gk = gk_ref[...]
    gq = gq_ref[...] * (HD ** -0.5)
    kbs, vbs = [], []
    for g in range(KVH):
        gs = slice(g * HD, (g + 1) * HD)
        k = _rms(k_ref[:, gs], gk)
        v = v_ref[:, gs]
        ko_ref[:, gs] = k
        vo_ref[:, gs] = v
        kbs.append(k.astype(BF))
        vbs.append(v.astype(BF))

    def scores(h):
        return [_mm_nt(_rms(q_ref[:, h * HD:(h + 1) * HD], gq), kbs[h // GQ])]

    def finish(h, s):
        o_ref[:, h * HD:(h + 1) * HD] = _softmax_pv(s, [vbs[h // GQ]]).astype(BF)

    _pipelined(list(range(NH)), scores, finish)


def _att_ctx_call(proj, g_q, g_k, l, caches):
    vec = pl.BlockSpec((1, HD), lambda b: (0, 0))
    kv_w = KVH * HD
    cache_spec = pl.BlockSpec((None, None, SEQ, kv_w), lambda b: (b, l, 0, 0))
    cache_shape = jax.ShapeDtypeStruct((BATCH, DEPTH, SEQ, kv_w), F32)
    in_specs = [
        pl.BlockSpec((SEQ, D), lambda b: (b, CB_AQ // NH)),
        pl.BlockSpec((SEQ, kv_w), lambda b: (b, CB_AK // KVH)),
        pl.BlockSpec((SEQ, kv_w), lambda b: (b, CB_AV // KVH)),
        vec, vec,
    ]
    args = [proj, proj, proj, g_q, g_k]
    aliases = {}
    if caches is not None:
        aliases = {len(args): 1, len(args) + 1: 2}
        in_specs += [pl.BlockSpec(memory_space=pl.ANY)] * 2
        args += list(caches)
    return pl.pallas_call(
        _att_ctx_kernel,
        grid=(BATCH,),
        in_specs=in_specs,
        out_specs=[pl.BlockSpec((SEQ, D), lambda b: (b, 0)), cache_spec, cache_spec],
        out_shape=[jax.ShapeDtypeStruct((T_CTX, D), BF), cache_shape, cache_shape],
        input_output_aliases=aliases,
        compiler_params=_params("parallel"),
        name="attention_ctx",
    )(*args)


QB = 256


def _att_smp_kernel(q_ref, k_ref, v_ref, ck_ref, cv_ref, gq_ref, gk_ref, cos_ref, sin_ref, o_ref):
    kb = _rope(_rms(k_ref[...], gk_ref[...]), cos_ref[...], sin_ref[...]).astype(BF)
    ckb = ck_ref[...].astype(BF)
    vs = [cv_ref[...].astype(BF), v_ref[...].astype(BF)]
    gq = gq_ref[...] * (HD ** -0.5)

    def scores(item):
        i, g = item
        rows = slice(i * QB, (i + 1) * QB)
        q = _rope(_rms(q_ref[rows, g * HD:(g + 1) * HD], gq), cos_ref[rows, :], sin_ref[rows, :]).astype(BF)
        return [_mm_nt(q, ckb), _mm_nt(q, kb)]

    def finish(item, s):
        i, g = item
        o_ref[i * QB:(i + 1) * QB, g * HD:(g + 1) * HD] = _softmax_pv(s, vs).astype(BF)

    _pipelined([(i, g) for i in range(DEC_SEQ // QB) for g in range(GQ)], scores, finish)


def _att_smp_call(proj, cache_k4, cache_v4, g_q, g_k, cos_t, sin_t, l):
    smp_blk0 = T_CTX // DEC_SEQ
    vec = pl.BlockSpec((1, HD), lambda b, g: (0, 0))
    table = pl.BlockSpec((DEC_SEQ, HD), lambda b, g: (0, 0))
    cache = pl.BlockSpec((None, None, PAST, HD), lambda b, g: (b, l, 0, g))
    return pl.pallas_call(
        _att_smp_kernel,
        grid=(DEC_BATCH, KVH),
        in_specs=[
            pl.BlockSpec((DEC_SEQ, GQ * HD), lambda b, g: (smp_blk0 + b, CB_AQ // GQ + g)),
            pl.BlockSpec((DEC_SEQ, HD), lambda b, g: (smp_blk0 + b, CB_AK + g)),
            pl.BlockSpec((DEC_SEQ, HD), lambda b, g: (smp_blk0 + b, CB_AV + g)),
            cache, cache, vec, vec, table, table,
        ],
        out_specs=pl.BlockSpec((DEC_SEQ, GQ * HD), lambda b, g: (b, g)),
        out_shape=jax.ShapeDtypeStruct((T_SMP, D), BF),
        compiler_params=_params("parallel", "parallel"),
        name="attention_smp",
    )(proj, proj, proj, cache_k4, cache_v4, g_q, g_k, cos_t, sin_t)


def _pick(i, ctx_ref, smp_ref):
    return jnp.where(i < N_CTX_TILES, ctx_ref[...], smp_ref[...])


def _mix_kernel(x_ref, ga_ref, gb_ref, gc_ref, ac_ref, as_ref, bc_ref, bs_ref, cc_ref, cs_ref,
                wa_ref, wb_ref, wc_ref, wo_ref, gpost_ref, gt_ref, o_ref):
    i = pl.program_id(0)
    m = jax.nn.sigmoid(ga_ref[...]) * jnp.dot(_pick(i, ac_ref, as_ref), wa_ref[...], preferred_element_type=F32)
    m += jax.nn.sigmoid(gb_ref[...]) * jnp.dot(_pick(i, bc_ref, bs_ref), wb_ref[...], preferred_element_type=F32)
    m += jax.nn.sigmoid(gc_ref[...]) * jnp.dot(_pick(i, cc_ref, cs_ref), wc_ref[...], preferred_element_type=F32)
    mix = jnp.dot(m.astype(BF), wo_ref[...], preferred_element_type=F32)
    o_ref[...] = x_ref[...] + gt_ref[...] * _rms(mix, gpost_ref[...])


def _resident(shape, index_map):
    return pl.BlockSpec(shape, index_map, pipeline_mode=pl.Buffered(1))


def _mix_call(x, proj, branches, weights, gain3, mod4, l):
    ctx_spec = pl.BlockSpec((TM, D), lambda i: (jnp.minimum(i, N_CTX_TILES - 1), 0))
    smp_spec = pl.BlockSpec((TM, D), lambda i: (jnp.maximum(i - N_CTX_TILES, 0), 0))
    wspec = _resident((None, D, D), lambda i: (l, 0, 0))

    def gate_spec(k):
        return pl.BlockSpec((TM, D), lambda i: (i, CB_MG // NH + k))

    in_specs = [pl.BlockSpec((TM, D), lambda i: (i, 0)), gate_spec(0), gate_spec(1), gate_spec(2)]
    args = [x, proj, proj, proj]
    for ctx, smp in branches:
        in_specs += [ctx_spec, smp_spec]
        args += [ctx, smp]
    in_specs += [wspec] * 4 + [_gain_spec(l), _mod_spec(l, 2)]
    args += list(weights) + [gain3, mod4]
    return pl.pallas_call(
        _mix_kernel,
        grid=(N_TILES,),
        in_specs=in_specs,
        out_specs=pl.BlockSpec((TM, D), lambda i: (i, 0)),
        out_shape=jax.ShapeDtypeStruct((T_ALL, D), F32),
        compiler_params=_params("parallel"),
        name="merge_outproj",
    )(*args)


def _mlp_kernel(x_ref, gpre_ref, sh_ref, sc_ref, wu_ref, wd_ref, gpost_ref, gt_ref, *rest, last):
    x = x_ref[...]
    h = _rms(x, gpre_ref[...]) * (1.0 + sc_ref[...]) + sh_ref[...]
    u = jnp.maximum(jnp.dot(h.astype(BF), wu_ref[...], preferred_element_type=F32), 0.0)
    dn = jnp.dot((u * u).astype(BF), wd_ref[...], preferred_element_type=F32)
    y = x + gt_ref[...] * _rms(dn, gpost_ref[...])
    if last:
        yc_ref, ys_ref = rest
        i = pl.program_id(0)

        @pl.when(i < N_CTX_TILES)
        def _():
            yc_ref[...] = y

        @pl.when(i >= N_CTX_TILES)
        def _():
            ys_ref[...] = y
    else:
        gn_ref, shn_ref, scn_ref, y_ref, hn_ref = rest
        y_ref[...] = y
        hn_ref[...] = (_rms(y, gn_ref[...]) * (1.0 + scn_ref[...]) + shn_ref[...]).astype(BF)


def _mlp_call(x, w_up, w_down, gpre3, gpost3, gpre_mix3, mod4, l):
    last = l == DEPTH - 1
    row_spec = pl.BlockSpec((TM, D), lambda i: (i, 0))
    in_specs = [
        row_spec,
        _gain_spec(l), _mod_spec(l, 3), _mod_spec(l, 4),
        _resident((None, D, MLP_H), lambda i: (l, 0, 0)),
        _resident((None, MLP_H, D), lambda i: (l, 0, 0)),
        _gain_spec(l), _mod_spec(l, 5),
    ]
    args = [x, gpre3, mod4, mod4, w_up, w_down, gpost3, mod4]
    if last:
        out_specs = [pl.BlockSpec((TM, D), lambda i: (jnp.minimum(i, N_CTX_TILES - 1), 0)),
                     pl.BlockSpec((TM, D), lambda i: (jnp.maximum(i - N_CTX_TILES, 0), 0))]
        out_shape = [jax.ShapeDtypeStruct((T_CTX, D), F32), jax.ShapeDtypeStruct((T_SMP, D), F32)]
    else:
        in_specs += [_gain_spec(l + 1), _mod_spec(l + 1, 0), _mod_spec(l + 1, 1)]
        args += [gpre_mix3, mod4, mod4]
        out_specs = [row_spec, row_spec]
        out_shape = [jax.ShapeDtypeStruct((T_ALL, D), F32), jax.ShapeDtypeStruct((T_ALL, D), BF)]
    return pl.pallas_call(
        functools.partial(_mlp_kernel, last=last),
        grid=(N_TILES,),
        in_specs=in_specs,
        out_specs=out_specs,
        out_shape=out_shape,
        compiler_params=_params("arbitrary"),
        name="mlp",
    )(*args)


def _rope_tables():
    t = jnp.arange(DEC_SEQ)
    n_freq = HD // 4
    freqs = ROPE_THETA ** (-jnp.arange(n_freq, dtype=F32) / n_freq)
    ang_r = (t // GRID_W).astype(F32)[:, None] * freqs
    ang_c = (t % GRID_W).astype(F32)[:, None] * freqs
    cos_t = jnp.concatenate([jnp.cos(ang_r)] * 2 + [jnp.cos(ang_c)] * 2, axis=1)
    sin_t = jnp.concatenate([-jnp.sin(ang_r), jnp.sin(ang_r), -jnp.sin(ang_c), jnp.sin(ang_c)], axis=1)
    return cos_t, sin_t


def _gate_lanes(p):
    flat = p.reshape(DEPTH, 1, 2 * NH)
    return jnp.pad(flat, ((0, 0), (0, 0), (2 * NH, LANE - 4 * NH)))


def kernel(x_prompt, x_sample, cache_k, cache_v, state_dn, c, c_ctx, w_mod, b_mod, g_pre_mix, g_post_mix, g_pre_mlp, g_post_mlp, w_in, w_conv_dn, a_log, dt_bias, g_dn_out, w_conv_sc, g_q, g_k, w_br_dn, w_br_sc, w_br_att, w_out, w_mlp_up, w_mlp_down):
    x = jnp.concatenate([x_prompt.reshape(T_CTX, D), x_sample.reshape(T_SMP, D)], axis=0)

    cond8 = jnp.zeros((8, D), F32).at[0].set(c_ctx).at[1:1 + DEC_BATCH].set(c)
    mod4 = _mod_call(cond8, w_mod, b_mod).reshape(DEPTH, 8, 1, 6 * D)

    w_a, w_b, w_c, w_o = (w.astype(BF) for w in (w_br_dn, w_br_sc, w_br_att, w_out))
    w_up, w_down = w_mlp_up.astype(BF), w_mlp_down.astype(BF)

    gpm3, gqm3 = g_pre_mix.reshape(DEPTH, 1, D), g_post_mix.reshape(DEPTH, 1, D)
    gpl3, gql3 = g_pre_mlp.reshape(DEPTH, 1, D), g_post_mlp.reshape(DEPTH, 1, D)
    alog_p, dtb_p = _gate_lanes(a_log), _gate_lanes(dt_bias)
    cos_t, sin_t = _rope_tables()
    cache_k4 = cache_k.reshape(DEC_BATCH, DEPTH, PAST, KVH * HD)
    cache_v4 = cache_v.reshape(DEC_BATCH, DEPTH, PAST, KVH * HD)

    w_t = jnp.swapaxes(w_in, 1, 2)

    caches, new_s = None, None
    h = _pre_call(x, gpm3, mod4, 0, 0, 1)
    for l in range(DEPTH):
        proj = _proj_call(h, w_t, l)

        gdn = g_dn_out[l].reshape(1, HD)
        dn_ctx, new_s = _dn_call(proj, w_conv_dn, alog_p[l], dtb_p[l], gdn, None, l,
                                 n_batch=BATCH, L=SEQ, row_blk0=0, emit_state=True, prev_states=new_s)
        (dn_smp,) = _dn_call(proj, w_conv_dn, alog_p[l], dtb_p[l], gdn, state_dn, l,
                             n_batch=DEC_BATCH, L=DEC_SEQ, row_blk0=T_CTX // DEC_SEQ, emit_state=False)
        sc_ctx = _sc_call(proj, w_conv_sc, l, n_batch=BATCH, L=SEQ, row_blk0=0)
        sc_smp = _sc_call(proj, w_conv_sc, l, n_batch=DEC_BATCH, L=DEC_SEQ, row_blk0=T_CTX // DEC_SEQ)
        gq, gk = g_q[l].reshape(1, HD), g_k[l].reshape(1, HD)
        at_ctx, *caches = _att_ctx_call(proj, gq, gk, l, caches)
        at_smp = _att_smp_call(proj, cache_k4, cache_v4, gq, gk, cos_t, sin_t, l)

        x = _mix_call(x, proj, [(dn_ctx, dn_smp), (sc_ctx, sc_smp), (at_ctx, at_smp)],
                      (w_a, w_b, w_c, w_o), gqm3, mod4, l)
        x, h = _mlp_call(x, w_up, w_down, gpl3, gql3, gpm3, mod4, l)

    y_ctx, y_smp = x, h
    new_k, new_v = (t.reshape(BATCH, DEPTH, SEQ, KVH, HD) for t in caches)
    return (y_ctx.reshape(BATCH, SEQ, D), y_smp.reshape(DEC_BATCH, DEC_SEQ, D), new_k, new_v, new_s)
```

```python
import functools

import jax
import jax.numpy as jnp
from jax import lax
from jax.experimental import pallas as pl
from jax.experimental.pallas import tpu as pltpu

D = 1024
BATCH, SEQ = 16, 256
DEC_BATCH, DEC_SEQ = 2, 1024
DEPTH = 4
PAST = 256
GRID_W = 64
NH = 8
HD = 128
KVH = 2
GQ = NH // KVH
CHUNK = 64
MLP_H = 4 * D
EPS = 1e-6
ROPE_THETA = 10000.0

T_CTX = BATCH * SEQ
T_SMP = DEC_BATCH * DEC_SEQ
T_ALL = T_CTX + T_SMP
TM = 512
N_CTX_TILES = T_CTX // TM
N_TILES = T_ALL // TM
TILES_PER_SMP = DEC_SEQ // TM

LANE = 128
CB_DQ, CB_DK, CB_DV, CB_DG, CB_SB, CB_SCG, CB_SX, CB_AQ, CB_MG, CB_AK, CB_AV, CB_GATE = 0, 8, 16, 24, 32, 40, 48, 56, 64, 88, 90, 92
N_CB = 96
PROJ_W = N_CB * LANE
PROJ_TN = 4 * LANE
ROW_CHUNK = 2048

VMEM_LIMIT = 56 * 1024 * 1024

BF = jnp.bfloat16
F32 = jnp.float32
HIGHEST = lax.Precision.HIGHEST


def _params(*sem):
    return pltpu.CompilerParams(dimension_semantics=sem, vmem_limit_bytes=VMEM_LIMIT)


def _mm(a, b):
    return jnp.dot(a.astype(BF), b.astype(BF), preferred_element_type=F32)


def _mm_nt(a, b, precision=None):
    if precision is None:
        a, b = a.astype(BF), b.astype(BF)
    return lax.dot_general(a, b, (((1,), (1,)), ((), ())), precision=precision, preferred_element_type=F32)


def _split(x):
    hi = x.astype(BF)
    return hi, (x - hi.astype(F32)).astype(BF)


def _mm_hi(a, b):
    return jnp.dot(a, b, precision=HIGHEST, preferred_element_type=F32)


def _rms(x, g):
    return x * lax.rsqrt(jnp.mean(x * x, axis=-1, keepdims=True) + EPS) * g


def _cond_row(i):
    return jnp.where(i < N_CTX_TILES, 0, 1 + (i - N_CTX_TILES) // TILES_PER_SMP)


def _mod_kernel(c_ref, w_ref, b_ref, o_ref):
    c = c_ref[...]
    s = c * jax.nn.sigmoid(c)
    o_ref[...] = _mm_hi(s, w_ref[...]) + b_ref[...]


def _mod_call(cond8, w_mod, b_mod):
    tn = 1536
    return pl.pallas_call(
        _mod_kernel,
        grid=(DEPTH, 6 * D // tn),
        in_specs=[
            pl.BlockSpec((8, D), lambda l, j: (0, 0)),
            pl.BlockSpec((None, D, tn), lambda l, j: (l, 0, j)),
            pl.BlockSpec((None, 1, tn), lambda l, j: (l, 0, j)),
        ],
        out_specs=pl.BlockSpec((None, 8, tn), lambda l, j: (l, 0, j)),
        out_shape=jax.ShapeDtypeStruct((DEPTH, 8, 6 * D), F32),
        compiler_params=_params("parallel", "parallel"),
        name="mod",
    )(cond8, w_mod, b_mod.reshape(DEPTH, 1, 6 * D))


def _mod_spec(l, chunk):
    return pl.BlockSpec((None, None, 1, D), lambda i: (l, _cond_row(i), 0, chunk))


def _gain_spec(l):
    return pl.BlockSpec((None, 1, D), lambda i: (l, 0, 0))


def _pre_kernel(x_ref, g_ref, sh_ref, sc_ref, h_ref):
    h = _rms(x_ref[...], g_ref[...]) * (1.0 + sc_ref[...]) + sh_ref[...]
    h_ref[...] = h.astype(BF)


def _pre_call(x, gain3, mod4, l, sh_chunk, sc_chunk):
    return pl.pallas_call(
        _pre_kernel,
        grid=(N_TILES,),
        in_specs=[
            pl.BlockSpec((TM, D), lambda i: (i, 0)),
            _gain_spec(l),
            _mod_spec(l, sh_chunk),
            _mod_spec(l, sc_chunk),
        ],
        out_specs=pl.BlockSpec((TM, D), lambda i: (i, 0)),
        out_shape=jax.ShapeDtypeStruct((T_ALL, D), BF),
        compiler_params=_params("parallel"),
        name="prenorm",
    )(x, gain3, mod4, mod4)


GATE_W = 4 * NH
N_PLAIN = 8
N_SHIFT = 15
J_GATE = N_PLAIN + N_SHIFT
J_AKAV = N_PLAIN + 8
IN_TOTAL = N_PLAIN * PROJ_TN + GATE_W + N_SHIFT * PROJ_TN


def _proj_src(j):
    sub = 8
    return sub * jnp.where(j < N_PLAIN, j * (PROJ_TN // sub),
                           jnp.where(j < J_GATE, j * (PROJ_TN // sub) + GATE_W // sub, N_PLAIN * PROJ_TN // sub))


def _proj_dst(j):
    return jnp.where(j < J_AKAV, j, jnp.where(j == J_AKAV, CB_AK // 4, jnp.where(j < J_GATE, j - 1, J_GATE)))


N_CAST = 16
CAST_W = MLP_H // N_CAST


def _proj_kernel(h_ref, wt_ref, wu_ref, wd_ref, o_ref, wu_o, wd_o, w_s):
    j = pl.program_id(0)
    w_s[...] = wt_ref[0].astype(BF)

    @pl.when(j < N_CAST)
    def _():
        wu_o[...] = wu_ref[...].astype(BF)
        wd_o[...] = wd_ref[...].astype(BF)

    def rows(width):
        def body(r, carry):
            r0 = pl.multiple_of(r * ROW_CHUNK, ROW_CHUNK)
            o_ref[pl.ds(r0, ROW_CHUNK), :width] = _mm_nt(h_ref[pl.ds(r0, ROW_CHUNK), :], w_s[:width, :])
            return carry
        lax.fori_loop(0, T_ALL // ROW_CHUNK, body, 0)

    @pl.when(j < J_GATE)
    def _():
        rows(PROJ_TN)

    @pl.when(j == J_GATE)
    def _():
        rows(LANE)

        def zero(r, carry):
            r0 = pl.multiple_of(r * TM, TM)
            o_ref[pl.ds(r0, TM), LANE:] = jnp.zeros((TM, PROJ_TN - LANE), F32)
            return carry
        lax.fori_loop(0, T_ALL // TM, zero, 0)


def _proj_call(h, w_t, w_up, w_down, l):
    assert w_t.shape == (DEPTH, IN_TOTAL, D)
    slab = lambda j: jnp.minimum(j, N_CAST - 1)
    return pl.pallas_call(
        _proj_kernel,
        grid=(J_GATE + 1,),
        in_specs=[
            pl.BlockSpec((T_ALL, D), lambda j: (0, 0), pipeline_mode=pl.Buffered(1)),
            pl.BlockSpec((pl.Element(1), pl.Element(PROJ_TN), pl.Element(D)), lambda j: (l, _proj_src(j), 0)),
            pl.BlockSpec((None, D, CAST_W), lambda j: (l, 0, slab(j))),
            pl.BlockSpec((None, CAST_W, D), lambda j: (l, slab(j), 0)),
        ],
        out_specs=[
            pl.BlockSpec((T_ALL, PROJ_TN), lambda j: (0, _proj_dst(j))),
            pl.BlockSpec((D, CAST_W), lambda j: (0, slab(j))),
            pl.BlockSpec((CAST_W, D), lambda j: (slab(j), 0)),
        ],
        out_shape=[
            jax.ShapeDtypeStruct((T_ALL, PROJ_W), F32),
            jax.ShapeDtypeStruct((D, MLP_H), BF),
            jax.ShapeDtypeStruct((MLP_H, D), BF),
        ],
        scratch_shapes=[pltpu.VMEM((PROJ_TN, D), BF)],
        compiler_params=_params("arbitrary"),
        name="inproj",
    )(h, w_t, w_up, w_down)


DN_HEADS = 4
DN_CHAINS = 16
DN_UNROLL = 2


def _dn_kernel(*refs, L, hg, has_s0, has_prev, emit_state):
    refs = list(refs)
    dq_ref, dk_ref, dv_ref, dg_ref, gt_ref, wq_ref, wk_ref, wv_ref, alog_ref, dtb_ref, gdn_ref = refs[:11]
    pos = 11
    s0_ref = None
    if has_s0:
        s0_ref = refs[pos]
        pos += 1
    if has_prev:
        pos += 1
    o_ref = refs[pos]
    pos += 1
    sfin_ref = None
    if emit_state:
        sfin_ref = refs[pos]
        pos += 1
    a_s, b_s, c_s, gc_s, bt_s, u_s, wq_s, qkkd_s, ge_s, s_s = refs[pos:]

    head0 = pl.program_id(1) * hg
    n_chunks = L // CHUNK
    W = hg * HD
    row = lax.broadcasted_iota(jnp.int32, (L, W), 0)

    def conv_silu(x_ref, w_ref):
        x = x_ref[...]
        w = w_ref[...]
        xm = jnp.where(row == 0, 0.0, pltpu.roll(x, 1, 0))
        xp = jnp.where(row == L - 1, 0.0, pltpu.roll(x, L - 1, 0))
        y = xm * w[0:1] + x * w[1:2] + xp * w[2:3]
        return y * jax.nn.sigmoid(y)

    def l2n(x):
        return x * lax.rsqrt(jnp.sum(x * x, axis=-1, keepdims=True) + EPS)

    q = conv_silu(dq_ref, wq_ref)
    k = conv_silu(dk_ref, wk_ref)
    for h in range(hg):
        hs = slice(h * HD, (h + 1) * HD)
        a_s[:, hs] = l2n(q[:, hs]) * (HD ** -0.5)
        b_s[:, hs] = l2n(k[:, hs])
    c_s[...] = conv_silu(dv_ref, wv_ref)

    grow = lax.broadcasted_iota(jnp.int32, (L, LANE), 0)
    glane = lax.broadcasted_iota(jnp.int32, (L, LANE), 1)
    cpos = grow & (CHUNK - 1)
    gates = gt_ref[...]
    bt_s[...] = jax.nn.sigmoid(gates)
    z = gates + dtb_ref[...]
    softplus = jnp.maximum(z, 0.0) + jnp.log(1.0 + jnp.exp(-jnp.abs(z)))
    g_all = -jnp.exp(alog_ref[...]) * softplus
    prefix, suffix = g_all, g_all
    for s in (1, 2, 4, 8, 16, 32):
        prefix = prefix + jnp.where(cpos >= s, pltpu.roll(prefix, s, 0), 0.0)
        suffix = suffix + jnp.where(cpos < CHUNK - s, pltpu.roll(suffix, L - s, 0), 0.0)
    gc_s[...] = jnp.where(glane < 3 * NH, prefix, suffix)

    ii = lax.broadcasted_iota(jnp.int32, (CHUNK, 2 * CHUNK), 0)
    jj = lax.broadcasted_iota(jnp.int32, (CHUNK, 2 * CHUNK), 1) & (CHUNK - 1)
    clane = lax.broadcasted_iota(jnp.int32, (CHUNK, LANE), 1)
    diag = (ii >> 1) == (jj >> 1)
    off_masks = [((ii >> bits) ^ (jj >> bits)) == 1 for bits in range(1, 6)]

    def column(arr, c):
        picked = jnp.sum(jnp.where(clane == c, arr, 0.0), axis=1, keepdims=True)
        return jnp.broadcast_to(picked, (CHUNK, HD))

    chains = [(h, d) for h in range(hg) for d in range(2)]

    chunks_per_iter = DN_CHAINS // (2 * hg)

    def pre_body(m, carry):
        group = []
        for c in range(chunks_per_iter):
            n = m * chunks_per_iter + c
            r0 = pl.multiple_of(n * CHUNK, CHUNK)
            gcr = gc_s[pl.ds(r0, CHUNK), :]
            btr = bt_s[pl.ds(r0, CHUNK), :]
            for h in range(hg):
                qc = a_s[pl.ds(r0, CHUNK), h * HD:(h + 1) * HD]
                kc = b_s[pl.ds(r0, CHUNK), h * HD:(h + 1) * HD]
                raw = _mm_nt(jnp.concatenate([kc, qc], axis=0),
                             jnp.concatenate([kc, kc], axis=0))
                group += [(h, d, n, r0, gcr, btr, qc, kc, raw) for d in range(2)]
        pre_group(group)
        return carry

    def pre_group(group):
        a_l, x_l = [], []
        for h, d, n, r0, gcr, btr, qc, kc, raw in group:
            incl = (ii >= jj) if d == 0 else (ii <= jj)
            strict = (ii > jj) if d == 0 else (ii < jj)
            col = d * NH + head0 + h
            bb = column(btr, col)
            gcb = column(gcr, 2 * NH + col)
            gct = jnp.transpose(jnp.concatenate([gcb, gcb], axis=0))[:CHUNK, :]
            decay = jnp.where(incl, jnp.exp(jnp.where(incl, gcb - gct, 0.0)), 0.0)
            eg = jnp.exp(gcb)
            gend = gcb[CHUNK - 1:CHUNK, :] if d == 0 else gcb[0:1, :]
            vc = c_s[pl.ds(r0, CHUNK), h * HD:(h + 1) * HD]
            wq_s[d, h, n, CHUNK:, :] = (qc * eg).astype(BF)
            qkkd_s[d, h, n, 0:CHUNK, :] = (raw[CHUNK:, :CHUNK] * decay[:, :CHUNK]).astype(BF)
            qkkd_s[d, h, n, CHUNK:, :] = jnp.transpose(kc * jnp.exp(gend - gcb)).astype(BF)
            ge_s[d, h, n] = jnp.broadcast_to(jnp.exp(gend), (8, HD))
            a_l.append(jnp.where(strict, bb * raw[:CHUNK] * decay, 0.0))
            x_l.append(jnp.concatenate([vc * bb, kc * (bb * eg)], axis=1))
        def each(fn, *lists):
            return [fn(*vals) for vals in zip(*lists)]

        mm1 = lambda a, b: jnp.dot(a.astype(BF)[:, :CHUNK], b.astype(BF), preferred_element_type=F32)
        y_l = [jnp.where(ii == jj, 1.0, 0.0) - jnp.where(diag, a, 0.0) for a in a_l]
        for off_diag in off_masks:
            z_l = each(lambda t, a: mm1(t, jnp.where(off_diag, a, 0.0)), y_l, a_l)
            y_l = each(lambda t, z: t - mm1(z, t), y_l, z_l)

        def apply3(t, x):
            th, tl = _split(t)
            xh, xl = _split(x)
            return jnp.dot(jnp.concatenate([th, tl[:, :CHUNK]], axis=1),
                           jnp.concatenate([xh, xl, xh], axis=0), preferred_element_type=F32)
        x_l = each(apply3, y_l, x_l)
        for (h, d, n, r0, *_), x in zip(group, x_l):
            u_s[d, h, pl.ds(r0, CHUNK), :] = x[:, :HD]
            wq_s[d, h, n, 0:CHUNK, :] = x[:, HD:].astype(BF)

    lax.fori_loop(0, n_chunks // chunks_per_iter, pre_body, 0, unroll=DN_UNROLL)

    for h, d in chains:
        s_s[d, h] = s0_ref[d, h] if has_s0 else jnp.zeros((HD, HD), F32)

    def scan_body(n, carry):
        cs = [n, n_chunks - 1 - n]
        r0s = [pl.multiple_of(c * CHUNK, CHUNK) for c in cs]
        s_l = [s_s[d, h] for h, d in chains]
        ws_l = [jnp.dot(wq_s[d, h, cs[d]], s.astype(BF), preferred_element_type=F32)
                for (h, d), s in zip(chains, s_l)]
        v_l = [u_s[d, h, pl.ds(r0s[d], CHUNK), :] - ws[:CHUNK] for (h, d), ws in zip(chains, ws_l)]
        kv_l = [jnp.dot(qkkd_s[d, h, cs[d]], v.astype(BF), preferred_element_type=F32)
                for (h, d), v in zip(chains, v_l)]
        for (h, d), s, ws, kv in zip(chains, s_l, ws_l, kv_l):
            s_s[d, h] = s * ge_s[d, h, cs[d]][0:1] + kv[CHUNK:]
            out_s = a_s if d == 0 else b_s
            out_s[pl.ds(r0s[d], CHUNK), h * HD:(h + 1) * HD] = ws[CHUNK:] + kv[:CHUNK]
        return carry

    lax.fori_loop(0, n_chunks, scan_body, 0, unroll=4)

    if emit_state:
        for d in range(2):
            for h in range(hg):
                sfin_ref[d, h] = s_s[d, h]

    gdn = gdn_ref[...]
    for h in range(hg):
        hs = slice(h * HD, (h + 1) * HD)
        dg = dg_ref[:, hs]
        o_ref[:, hs] = (_rms(a_s[:, hs] + b_s[:, hs], gdn) * (dg * jax.nn.sigmoid(dg))).astype(BF)


def _dn_call(proj, w_conv, alog_l, dtb_l, gdn, state, l, *, n_batch, L, row_blk0, emit_state, prev_states=None):
    has_s0 = state is not None
    has_prev = prev_states is not None
    hg = DN_HEADS
    n_chunks = L // CHUNK
    W = hg * HD

    def pspec(cb):
        return pl.BlockSpec((L, W), lambda b, j: (row_blk0 + b, cb // hg + j))

    def wspec(off):
        return pl.BlockSpec((None, 3, W), lambda b, j: (l, 0, off // hg + j))

    vec = pl.BlockSpec((1, HD), lambda b, j: (0, 0))
    in_specs = [pspec(CB_DQ), pspec(CB_DK), pspec(CB_DV), pspec(CB_DG),
                pl.BlockSpec((L, LANE), lambda b, j: (row_blk0 + b, CB_GATE)),
                wspec(0), wspec(NH), wspec(2 * NH), vec, vec, vec]
    args = [proj, proj, proj, proj, proj, w_conv, w_conv, w_conv, alog_l, dtb_l, gdn]
    if has_s0:
        in_specs.append(pl.BlockSpec((None, None, 2, hg, HD, HD), lambda b, j: (b, l, 0, j, 0, 0)))
        args.append(state)
    aliases = {}
    if has_prev:
        aliases = {len(args): 1}
        in_specs.append(pl.BlockSpec(memory_space=pl.ANY))
        args.append(prev_states)
    out_specs = [pl.BlockSpec((L, W), lambda b, j: (b, j))]
    out_shape = [jax.ShapeDtypeStruct((n_batch * L, D), BF)]
    if emit_state:
        out_specs.append(pl.BlockSpec((None, None, 2, hg, HD, HD), lambda b, j: (b, l, 0, j, 0, 0)))
        out_shape.append(jax.ShapeDtypeStruct((n_batch, DEPTH, 2, NH, HD, HD), F32))
    return pl.pallas_call(
        functools.partial(_dn_kernel, L=L, hg=hg, has_s0=has_s0, has_prev=has_prev, emit_state=emit_state),
        grid=(n_batch, NH // hg),
        in_specs=in_specs,
        out_specs=out_specs,
        out_shape=out_shape,
        input_output_aliases=aliases,
        scratch_shapes=[
            pltpu.VMEM((L, W), F32), pltpu.VMEM((L, W), F32), pltpu.VMEM((L, W), F32),
            pltpu.VMEM((L, LANE), F32), pltpu.VMEM((L, LANE), F32),
            pltpu.VMEM((2, hg, L, HD), F32),
            pltpu.VMEM((2, hg, n_chunks, 2 * CHUNK, HD), BF),
            pltpu.VMEM((2, hg, n_chunks, CHUNK + HD, CHUNK), BF),
            pltpu.VMEM((2, hg, n_chunks, 8, HD), F32),
            pltpu.VMEM((2, hg, HD, HD), F32),
        ],
        compiler_params=_params("parallel", "parallel"),
        name="deltanet_ctx" if not has_s0 else "deltanet_smp",
    )(*args)


def _sc_kernel(sb_ref, scg_ref, sx_ref, w_ref, o_ref, *, L):
    tn = o_ref.shape[1]
    row = lax.broadcasted_iota(jnp.int32, (L, tn), 0)
    x = scg_ref[...] * sx_ref[...]
    w = w_ref[...]
    xm = jnp.where(row == 0, 0.0, pltpu.roll(x, 1, 0))
    xp = jnp.where(row == L - 1, 0.0, pltpu.roll(x, L - 1, 0))
    y = xm * w[0:1] + x * w[1:2] + xp * w[2:3]
    o_ref[...] = (sb_ref[...] * y).astype(BF)


SC_BLOCK_ELEMS = 256 * 1024


def _sc_call(proj, w_conv_sc, l, *, n_batch, L, row_blk0):
    tn = SC_BLOCK_ELEMS // L

    def pspec(cb):
        return pl.BlockSpec((L, tn), lambda b, j: (row_blk0 + b, cb * LANE // tn + j))

    return pl.pallas_call(
        functools.partial(_sc_kernel, L=L),
        grid=(n_batch, D // tn),
        in_specs=[pspec(CB_SB), pspec(CB_SCG), pspec(CB_SX),
                  pl.BlockSpec((None, 3, tn), lambda b, j: (l, 0, j))],
        out_specs=pl.BlockSpec((L, tn), lambda b, j: (b, j)),
        out_shape=jax.ShapeDtypeStruct((n_batch * L, D), BF),
        compiler_params=_params("parallel", "parallel"),
        name="shortconv",
    )(proj, proj, proj, w_conv_sc)


def _rope(x, cos, sin_signed):
    lane = lax.broadcasted_iota(jnp.int32, x.shape, 1)
    swapped = jnp.where((lane & 63) < 32, pltpu.roll(x, HD - 32, 1), pltpu.roll(x, 32, 1))
    return x * cos + swapped * sin_signed


def _softmax_pv(s_parts, v_parts):
    m = s_parts[0].max(axis=-1, keepdims=True)
    for s in s_parts[1:]:
        m = jnp.maximum(m, s.max(axis=-1, keepdims=True))
    den = None
    acc = None
    for s, v in zip(s_parts, v_parts):
        p = jnp.exp(s - m)
        ps = jnp.sum(p, axis=-1, keepdims=True)
        pv = _mm(p, v)
        den = ps if den is None else den + ps
        acc = pv if acc is None else acc + pv
    return acc / den


def _pipelined(items, scores, finish):
    s_next = scores(items[0])
    for n, item in enumerate(items):
        s_cur = s_next
        if n + 1 < len(items):
            s_next = scores(items[n + 1])
        finish(item, s_cur)


def _att_ctx_kernel(q_ref, k_ref, v_ref, gq_ref, gk_ref, *rest):
    o_ref, ko_ref, vo_ref = rest[-3:]
    gk = gk_ref[...]
    gq = gq_ref[...] * (HD ** -0.5)
    kbs, vbs = [], []
    for g in range(KVH):
        gs = slice(g * HD, (g + 1) * HD)
        k = _rms(k_ref[:, gs], gk)
        v = v_ref[:, gs]
        ko_ref[:, gs] = k
        vo_ref[:, gs] = v
        kbs.append(k.astype(BF))
        vbs.append(v.astype(BF))

    def scores(h):
        return [_mm_nt(_rms(q_ref[:, h * HD:(h + 1) * HD], gq), kbs[h // GQ])]

    def finish(h, s):
        o_ref[:, h * HD:(h + 1) * HD] = _softmax_pv(s, [vbs[h // GQ]]).astype(BF)

    _pipelined(list(range(NH)), scores, finish)


def _att_ctx_call(proj, g_q, g_k, l, caches):
    vec = pl.BlockSpec((1, HD), lambda b: (0, 0))
    kv_w = KVH * HD
    cache_spec = pl.BlockSpec((None, None, SEQ, kv_w), lambda b: (b, l, 0, 0))
    cache_shape = jax.ShapeDtypeStruct((BATCH, DEPTH, SEQ, kv_w), F32)
    in_specs = [
        pl.BlockSpec((SEQ, D), lambda b: (b, CB_AQ // NH)),
        pl.BlockSpec((SEQ, kv_w), lambda b: (b, CB_AK // KVH)),
        pl.BlockSpec((SEQ, kv_w), lambda b: (b, CB_AV // KVH)),
        vec, vec,
    ]
    args = [proj, proj, proj, g_q, g_k]
    aliases = {}
    if caches is not None:
        aliases = {len(args): 1, len(args) + 1: 2}
        in_specs += [pl.BlockSpec(memory_space=pl.ANY)] * 2
        args += list(caches)
    return pl.pallas_call(
        _att_ctx_kernel,
        grid=(BATCH,),
        in_specs=in_specs,
        out_specs=[pl.BlockSpec((SEQ, D), lambda b: (b, 0)), cache_spec, cache_spec],
        out_shape=[jax.ShapeDtypeStruct((T_CTX, D), BF), cache_shape, cache_shape],
        input_output_aliases=aliases,
        compiler_params=_params("parallel"),
        name="attention_ctx",
    )(*args)


QB = 256


def _att_smp_kernel(q_ref, k_ref, v_ref, ck_ref, cv_ref, gq_ref, gk_ref, cos_ref, sin_ref, o_ref):
    kb = _rope(_rms(k_ref[...], gk_ref[...]), cos_ref[...], sin_ref[...]).astype(BF)
    ckb = ck_ref[...].astype(BF)
    vs = [cv_ref[...].astype(BF), v_ref[...].astype(BF)]
    gq = gq_ref[...] * (HD ** -0.5)

    def scores(item):
        i, g = item
        rows = slice(i * QB, (i + 1) * QB)
        q = _rope(_rms(q_ref[rows, g * HD:(g + 1) * HD], gq), cos_ref[rows, :], sin_ref[rows, :]).astype(BF)
        return [_mm_nt(q, ckb), _mm_nt(q, kb)]

    def finish(item, s):
        i, g = item
        o_ref[i * QB:(i + 1) * QB, g * HD:(g + 1) * HD] = _softmax_pv(s, vs).astype(BF)

    _pipelined([(i, g) for i in range(DEC_SEQ // QB) for g in range(GQ)], scores, finish)


def _att_smp_call(proj, cache_k4, cache_v4, g_q, g_k, cos_t, sin_t, l):
    smp_blk0 = T_CTX // DEC_SEQ
    vec = pl.BlockSpec((1, HD), lambda b, g: (0, 0))
    table = pl.BlockSpec((DEC_SEQ, HD), lambda b, g: (0, 0))
    cache = pl.BlockSpec((None, None, PAST, HD), lambda b, g: (b, l, 0, g))
    return pl.pallas_call(
        _att_smp_kernel,
        grid=(DEC_BATCH, KVH),
        in_specs=[
            pl.BlockSpec((DEC_SEQ, GQ * HD), lambda b, g: (smp_blk0 + b, CB_AQ // GQ + g)),
            pl.BlockSpec((DEC_SEQ, HD), lambda b, g: (smp_blk0 + b, CB_AK + g)),
            pl.BlockSpec((DEC_SEQ, HD), lambda b, g: (smp_blk0 + b, CB_AV + g)),
            cache, cache, vec, vec, table, table,
        ],
        out_specs=pl.BlockSpec((DEC_SEQ, GQ * HD), lambda b, g: (b, g)),
        out_shape=jax.ShapeDtypeStruct((T_SMP, D), BF),
        compiler_params=_params("parallel", "parallel"),
        name="attention_smp",
    )(proj, proj, proj, cache_k4, cache_v4, g_q, g_k, cos_t, sin_t)


def _pick(i, ctx_ref, smp_ref):
    return jnp.where(i < N_CTX_TILES, ctx_ref[...], smp_ref[...])


def _mix_kernel(x_ref, ga_ref, gb_ref, gc_ref, ac_ref, as_ref, bc_ref, bs_ref, cc_ref, cs_ref,
                wa_ref, wb_ref, wc_ref, wo_ref, gpost_ref, gt_ref, o_ref):
    i = pl.program_id(0)
    m = jax.nn.sigmoid(ga_ref[...]) * jnp.dot(_pick(i, ac_ref, as_ref), wa_ref[...], preferred_element_type=F32)
    m += jax.nn.sigmoid(gb_ref[...]) * jnp.dot(_pick(i, bc_ref, bs_ref), wb_ref[...], preferred_element_type=F32)
    m += jax.nn.sigmoid(gc_ref[...]) * jnp.dot(_pick(i, cc_ref, cs_ref), wc_ref[...], preferred_element_type=F32)
    mix = jnp.dot(m.astype(BF), wo_ref[...], preferred_element_type=F32)
    o_ref[...] = x_ref[...] + gt_ref[...] * _rms(mix, gpost_ref[...])


def _resident(shape, index_map):
    return pl.BlockSpec(shape, index_map, pipeline_mode=pl.Buffered(1))


def _mix_call(x, proj, branches, weights, gain3, mod4, l):
    ctx_spec = pl.BlockSpec((TM, D), lambda i: (jnp.minimum(i, N_CTX_TILES - 1), 0))
    smp_spec = pl.BlockSpec((TM, D), lambda i: (jnp.maximum(i - N_CTX_TILES, 0), 0))
    wspec = _resident((None, D, D), lambda i: (l, 0, 0))

    def gate_spec(k):
        return pl.BlockSpec((TM, D), lambda i: (i, CB_MG // NH + k))

    in_specs = [pl.BlockSpec((TM, D), lambda i: (i, 0)), gate_spec(0), gate_spec(1), gate_spec(2)]
    args = [x, proj, proj, proj]
    for ctx, smp in branches:
        in_specs += [ctx_spec, smp_spec]
        args += [ctx, smp]
    in_specs += [wspec] * 4 + [_gain_spec(l), _mod_spec(l, 2)]
    args += list(weights) + [gain3, mod4]
    return pl.pallas_call(
        _mix_kernel,
        grid=(N_TILES,),
        in_specs=in_specs,
        out_specs=pl.BlockSpec((TM, D), lambda i: (i, 0)),
        out_shape=jax.ShapeDtypeStruct((T_ALL, D), F32),
        compiler_params=_params("parallel"),
        name="merge_outproj",
    )(*args)


def _mlp_kernel(x_ref, gpre_ref, sh_ref, sc_ref, wu_ref, wd_ref, gpost_ref, gt_ref, *rest, last):
    x = x_ref[...]
    h = _rms(x, gpre_ref[...]) * (1.0 + sc_ref[...]) + sh_ref[...]
    u = jnp.maximum(jnp.dot(h.astype(BF), wu_ref[...], preferred_element_type=F32), 0.0)
    dn = jnp.dot((u * u).astype(BF), wd_ref[...], preferred_element_type=F32)
    y = x + gt_ref[...] * _rms(dn, gpost_ref[...])
    if last:
        yc_ref, ys_ref = rest
        i = pl.program_id(0)

        @pl.when(i < N_CTX_TILES)
        def _():
            yc_ref[...] = y

        @pl.when(i >= N_CTX_TILES)
        def _():
            ys_ref[...] = y
    else:
        gn_ref, shn_ref, scn_ref, y_ref, hn_ref = rest
        y_ref[...] = y
        hn_ref[...] = (_rms(y, gn_ref[...]) * (1.0 + scn_ref[...]) + shn_ref[...]).astype(BF)


def _mlp_call(x, w_up, w_down, gpre3, gpost3, gpre_mix3, mod4, l):
    last = l == DEPTH - 1
    row_spec = pl.BlockSpec((TM, D), lambda i: (i, 0))
    in_specs = [
        row_spec,
        _gain_spec(l), _mod_spec(l, 3), _mod_spec(l, 4),
        _resident((D, MLP_H), lambda i: (0, 0)),
        _resident((MLP_H, D), lambda i: (0, 0)),
        _gain_spec(l), _mod_spec(l, 5),
    ]
    args = [x, gpre3, mod4, mod4, w_up, w_down, gpost3, mod4]
    if last:
        out_specs = [pl.BlockSpec((TM, D), lambda i: (jnp.minimum(i, N_CTX_TILES - 1), 0)),
                     pl.BlockSpec((TM, D), lambda i: (jnp.maximum(i - N_CTX_TILES, 0), 0))]
        out_shape = [jax.ShapeDtypeStruct((T_CTX, D), F32), jax.ShapeDtypeStruct((T_SMP, D), F32)]
    else:
        in_specs += [_gain_spec(l + 1), _mod_spec(l + 1, 0), _mod_spec(l + 1, 1)]
        args += [gpre_mix3, mod4, mod4]
        out_specs = [row_spec, row_spec]
        out_shape = [jax.ShapeDtypeStruct((T_ALL, D), F32), jax.ShapeDtypeStruct((T_ALL, D), BF)]
    return pl.pallas_call(
        functools.partial(_mlp_kernel, last=last),
        grid=(N_TILES,),
        in_specs=in_specs,
        out_specs=out_specs,
        out_shape=out_shape,
        compiler_params=_params("arbitrary"),
        name="mlp",
    )(*args)


def _rope_tables():
    t = jnp.arange(DEC_SEQ)
    n_freq = HD // 4
    freqs = ROPE_THETA ** (-jnp.arange(n_freq, dtype=F32) / n_freq)
    ang_r = (t // GRID_W).astype(F32)[:, None] * freqs
    ang_c = (t % GRID_W).astype(F32)[:, None] * freqs
    cos_t = jnp.concatenate([jnp.cos(ang_r)] * 2 + [jnp.cos(ang_c)] * 2, axis=1)
    sin_t = jnp.concatenate([-jnp.sin(ang_r), jnp.sin(ang_r), -jnp.sin(ang_c), jnp.sin(ang_c)], axis=1)
    return cos_t, sin_t


def _gate_lanes(p):
    flat = p.reshape(DEPTH, 1, 2 * NH)
    return jnp.pad(flat, ((0, 0), (0, 0), (2 * NH, LANE - 4 * NH)))


def kernel(x_prompt, x_sample, cache_k, cache_v, state_dn, c, c_ctx, w_mod, b_mod, g_pre_mix, g_post_mix, g_pre_mlp, g_post_mlp, w_in, w_conv_dn, a_log, dt_bias, g_dn_out, w_conv_sc, g_q, g_k, w_br_dn, w_br_sc, w_br_att, w_out, w_mlp_up, w_mlp_down):
    x = jnp.concatenate([x_prompt.reshape(T_CTX, D), x_sample.reshape(T_SMP, D)], axis=0)

    cond8 = jnp.zeros((8, D), F32).at[0].set(c_ctx).at[1:1 + DEC_BATCH].set(c)
    mod4 = _mod_call(cond8, w_mod, b_mod).reshape(DEPTH, 8, 1, 6 * D)

    w_a, w_b, w_c, w_o = (w.astype(BF) for w in (w_br_dn, w_br_sc, w_br_att, w_out))

    gpm3, gqm3 = g_pre_mix.reshape(DEPTH, 1, D), g_post_mix.reshape(DEPTH, 1, D)
    gpl3, gql3 = g_pre_mlp.reshape(DEPTH, 1, D), g_post_mlp.reshape(DEPTH, 1, D)
    alog_p, dtb_p = _gate_lanes(a_log), _gate_lanes(dt_bias)
    cos_t, sin_t = _rope_tables()
    cache_k4 = cache_k.reshape(DEC_BATCH, DEPTH, PAST, KVH * HD)
    cache_v4 = cache_v.reshape(DEC_BATCH, DEPTH, PAST, KVH * HD)

    w_t = jnp.swapaxes(w_in, 1, 2)

    caches, new_s = None, None
    h = _pre_call(x, gpm3, mod4, 0, 0, 1)
    for l in range(DEPTH):
        proj, w_up, w_down = _proj_call(h, w_t, w_mlp_up, w_mlp_down, l)

        gdn = g_dn_out[l].reshape(1, HD)
        dn_ctx, new_s = _dn_call(proj, w_conv_dn, alog_p[l], dtb_p[l], gdn, None, l,
                                 n_batch=BATCH, L=SEQ, row_blk0=0, emit_state=True, prev_states=new_s)
        (dn_smp,) = _dn_call(proj, w_conv_dn, alog_p[l], dtb_p[l], gdn, state_dn, l,
                             n_batch=DEC_BATCH, L=DEC_SEQ, row_blk0=T_CTX // DEC_SEQ, emit_state=False)
        sc_ctx = _sc_call(proj, w_conv_sc, l, n_batch=BATCH, L=SEQ, row_blk0=0)
        sc_smp = _sc_call(proj, w_conv_sc, l, n_batch=DEC_BATCH, L=DEC_SEQ, row_blk0=T_CTX // DEC_SEQ)
        gq, gk = g_q[l].reshape(1, HD), g_k[l].reshape(1, HD)
        at_ctx, *caches = _att_ctx_call(proj, gq, gk, l, caches)
        at_smp = _att_smp_call(proj, cache_k4, cache_v4, gq, gk, cos_t, sin_t, l)

        x = _mix_call(x, proj, [(dn_ctx, dn_smp), (sc_ctx, sc_smp), (at_ctx, at_smp)],
                      (w_a, w_b, w_c, w_o), gqm3, mod4, l)
        x, h = _mlp_call(x, w_up, w_down, gpl3, gql3, gpm3, mod4, l)

    y_ctx, y_smp = x, h
    new_k, new_v = (t.reshape(BATCH, DEPTH, SEQ, KVH, HD) for t in caches)
    return (y_ctx.reshape(BATCH, SEQ, D), y_smp.reshape(DEC_BATCH, DEC_SEQ, D), new_k, new_v, new_s)
```

```python
import functools

import jax
import jax.numpy as jnp
from jax import lax
from jax.experimental import pallas as pl
from jax.experimental.pallas import tpu as pltpu

D = 1024
BATCH, SEQ = 16, 256
DEC_BATCH, DEC_SEQ = 2, 1024
DEPTH = 4
PAST = 256
GRID_W = 64
NH = 8
HD = 128
KVH = 2
GQ = NH // KVH
CHUNK = 64
MLP_H = 4 * D
EPS = 1e-6
ROPE_THETA = 10000.0

T_CTX = BATCH * SEQ
T_SMP = DEC_BATCH * DEC_SEQ
T_ALL = T_CTX + T_SMP
TM = 512
N_CTX_TILES = T_CTX // TM
N_TILES = T_ALL // TM
TILES_PER_SMP = DEC_SEQ // TM

LANE = 128
CB_DQ, CB_DK, CB_DV, CB_DG, CB_SB, CB_SCG, CB_SX, CB_AQ, CB_MG, CB_AK, CB_AV, CB_GATE = 0, 8, 16, 24, 32, 40, 48, 56, 64, 88, 90, 92
N_CB = 96
PROJ_W = N_CB * LANE
PROJ_TN = 4 * LANE
ROW_CHUNK = 2048

VMEM_LIMIT = 56 * 1024 * 1024

BF = jnp.bfloat16
F32 = jnp.float32
HIGHEST = lax.Precision.HIGHEST


def _params(*sem):
    return pltpu.CompilerParams(dimension_semantics=sem, vmem_limit_bytes=VMEM_LIMIT)


def _mm(a, b):
    return jnp.dot(a.astype(BF), b.astype(BF), preferred_element_type=F32)


def _mm_nt(a, b, precision=None):
    if precision is None:
        a, b = a.astype(BF), b.astype(BF)
    return lax.dot_general(a, b, (((1,), (1,)), ((), ())), precision=precision, preferred_element_type=F32)


def _split(x):
    hi = x.astype(BF)
    return hi, (x - hi.astype(F32)).astype(BF)


def _mm_hi(a, b):
    return jnp.dot(a, b, precision=HIGHEST, preferred_element_type=F32)


def _rms(x, g):
    return x * lax.rsqrt(jnp.mean(x * x, axis=-1, keepdims=True) + EPS) * g


def _cond_row(i):
    return jnp.where(i < N_CTX_TILES, 0, 1 + (i - N_CTX_TILES) // TILES_PER_SMP)


def _mod_kernel(c_ref, w_ref, b_ref, o_ref):
    c = c_ref[...]
    s = c * jax.nn.sigmoid(c)
    o_ref[...] = _mm_hi(s, w_ref[...]) + b_ref[...]


def _mod_call(cond8, w_mod, b_mod):
    tn = 1536
    return pl.pallas_call(
        _mod_kernel,
        grid=(DEPTH, 6 * D // tn),
        in_specs=[
            pl.BlockSpec((8, D), lambda l, j: (0, 0)),
            pl.BlockSpec((None, D, tn), lambda l, j: (l, 0, j)),
            pl.BlockSpec((None, 1, tn), lambda l, j: (l, 0, j)),
        ],
        out_specs=pl.BlockSpec((None, 8, tn), lambda l, j: (l, 0, j)),
        out_shape=jax.ShapeDtypeStruct((DEPTH, 8, 6 * D), F32),
        compiler_params=_params("parallel", "parallel"),
        name="mod",
    )(cond8, w_mod, b_mod.reshape(DEPTH, 1, 6 * D))


def _mod_spec(l, chunk):
    return pl.BlockSpec((None, None, 1, D), lambda i: (l, _cond_row(i), 0, chunk))


def _gain_spec(l):
    return pl.BlockSpec((None, 1, D), lambda i: (l, 0, 0))


def _pre_kernel(x_ref, g_ref, sh_ref, sc_ref, h_ref):
    h = _rms(x_ref[...], g_ref[...]) * (1.0 + sc_ref[...]) + sh_ref[...]
    h_ref[...] = h.astype(BF)


def _pre_call(x, gain3, mod4, l, sh_chunk, sc_chunk):
    return pl.pallas_call(
        _pre_kernel,
        grid=(N_TILES,),
        in_specs=[
            pl.BlockSpec((TM, D), lambda i: (i, 0)),
            _gain_spec(l),
            _mod_spec(l, sh_chunk),
            _mod_spec(l, sc_chunk),
        ],
        out_specs=pl.BlockSpec((TM, D), lambda i: (i, 0)),
        out_shape=jax.ShapeDtypeStruct((T_ALL, D), BF),
        compiler_params=_params("parallel"),
        name="prenorm",
    )(x, gain3, mod4, mod4)


GATE_W = 4 * NH
N_PLAIN = 8
N_SHIFT = 15
J_GATE = N_PLAIN + N_SHIFT
J_AKAV = N_PLAIN + 8
IN_TOTAL = N_PLAIN * PROJ_TN + GATE_W + N_SHIFT * PROJ_TN


def _proj_src(j):
    sub = 8
    return sub * jnp.where(j < N_PLAIN, j * (PROJ_TN // sub),
                           jnp.where(j < J_GATE, j * (PROJ_TN // sub) + GATE_W // sub, N_PLAIN * PROJ_TN // sub))


def _proj_dst(j):
    return jnp.where(j < J_AKAV, j, jnp.where(j == J_AKAV, CB_AK // 4, jnp.where(j < J_GATE, j - 1, J_GATE)))


N_CAST = 16
CAST_W = MLP_H // N_CAST


N_SIDE = 6


def _proj_kernel(h_ref, wt_ref, *rest):
    side_in, o_ref, side_out, w_s = rest[:N_SIDE], rest[N_SIDE], rest[N_SIDE + 1:2 * N_SIDE + 1], rest[-1]
    j = pl.program_id(0)
    w_s[...] = wt_ref[0].astype(BF)

    @pl.when(j < N_CAST)
    def _():
        for src, dst in zip(side_in, side_out):
            dst[...] = src[...].astype(BF)

    def rows(width):
        def body(r, carry):
            r0 = pl.multiple_of(r * ROW_CHUNK, ROW_CHUNK)
            o_ref[pl.ds(r0, ROW_CHUNK), :width] = _mm_nt(h_ref[pl.ds(r0, ROW_CHUNK), :], w_s[:width, :])
            return carry
        lax.fori_loop(0, T_ALL // ROW_CHUNK, body, 0)

    @pl.when(j < J_GATE)
    def _():
        rows(PROJ_TN)

    @pl.when(j == J_GATE)
    def _():
        rows(LANE)

        def zero(r, carry):
            r0 = pl.multiple_of(r * TM, TM)
            o_ref[pl.ds(r0, TM), LANE:] = jnp.zeros((TM, PROJ_TN - LANE), F32)
            return carry
        lax.fori_loop(0, T_ALL // TM, zero, 0)


def _proj_call(h, w_t, w_up, w_down, w_square, l):
    assert w_t.shape == (DEPTH, IN_TOTAL, D) and len(w_square) == N_SIDE - 2
    slab = lambda j: jnp.minimum(j, N_CAST - 1)
    rows = D // N_CAST
    side_in = [pl.BlockSpec((None, D, CAST_W), lambda j: (l, 0, slab(j))),
               pl.BlockSpec((None, CAST_W, D), lambda j: (l, slab(j), 0))]
    side_out = [pl.BlockSpec((D, CAST_W), lambda j: (0, slab(j))),
                pl.BlockSpec((CAST_W, D), lambda j: (slab(j), 0))]
    side_shape = [jax.ShapeDtypeStruct((D, MLP_H), BF), jax.ShapeDtypeStruct((MLP_H, D), BF)]
    for _ in w_square:
        side_in.append(pl.BlockSpec((None, rows, D), lambda j: (l, slab(j), 0)))
        side_out.append(pl.BlockSpec((rows, D), lambda j: (slab(j), 0)))
        side_shape.append(jax.ShapeDtypeStruct((D, D), BF))
    return pl.pallas_call(
        _proj_kernel,
        grid=(J_GATE + 1,),
        in_specs=[
            pl.BlockSpec((T_ALL, D), lambda j: (0, 0), pipeline_mode=pl.Buffered(1)),
            pl.BlockSpec((pl.Element(1), pl.Element(PROJ_TN), pl.Element(D)), lambda j: (l, _proj_src(j), 0)),
        ] + side_in,
        out_specs=[pl.BlockSpec((T_ALL, PROJ_TN), lambda j: (0, _proj_dst(j)))] + side_out,
        out_shape=[jax.ShapeDtypeStruct((T_ALL, PROJ_W), F32)] + side_shape,
        scratch_shapes=[pltpu.VMEM((PROJ_TN, D), BF)],
        compiler_params=_params("arbitrary"),
        name="inproj",
    )(h, w_t, w_up, w_down, *w_square)


DN_HEADS = 4
DN_CHAINS = 16
DN_UNROLL = 2


def _dn_kernel(*refs, L, hg, has_s0, has_prev, emit_state):
    refs = list(refs)
    dq_ref, dk_ref, dv_ref, dg_ref, gt_ref, wq_ref, wk_ref, wv_ref, alog_ref, dtb_ref, gdn_ref = refs[:11]
    pos = 11
    s0_ref = None
    if has_s0:
        s0_ref = refs[pos]
        pos += 1
    if has_prev:
        pos += 1
    o_ref = refs[pos]
    pos += 1
    sfin_ref = None
    if emit_state:
        sfin_ref = refs[pos]
        pos += 1
    a_s, b_s, c_s, gc_s, bt_s, u_s, wq_s, qkkd_s, ge_s, s_s = refs[pos:]

    head0 = pl.program_id(1) * hg
    n_chunks = L // CHUNK
    W = hg * HD
    row = lax.broadcasted_iota(jnp.int32, (L, W), 0)

    def conv_silu(x_ref, w_ref):
        x = x_ref[...]
        w = w_ref[...]
        xm = jnp.where(row == 0, 0.0, pltpu.roll(x, 1, 0))
        xp = jnp.where(row == L - 1, 0.0, pltpu.roll(x, L - 1, 0))
        y = xm * w[0:1] + x * w[1:2] + xp * w[2:3]
        return y * jax.nn.sigmoid(y)

    def l2n(x):
        return x * lax.rsqrt(jnp.sum(x * x, axis=-1, keepdims=True) + EPS)

    q = conv_silu(dq_ref, wq_ref)
    k = conv_silu(dk_ref, wk_ref)
    for h in range(hg):
        hs = slice(h * HD, (h + 1) * HD)
        a_s[:, hs] = l2n(q[:, hs]) * (HD ** -0.5)
        b_s[:, hs] = l2n(k[:, hs])
    c_s[...] = conv_silu(dv_ref, wv_ref)

    grow = lax.broadcasted_iota(jnp.int32, (L, LANE), 0)
    glane = lax.broadcasted_iota(jnp.int32, (L, LANE), 1)
    cpos = grow & (CHUNK - 1)
    gates = gt_ref[...]
    bt_s[...] = jax.nn.sigmoid(gates)
    z = gates + dtb_ref[...]
    softplus = jnp.maximum(z, 0.0) + jnp.log(1.0 + jnp.exp(-jnp.abs(z)))
    g_all = -jnp.exp(alog_ref[...]) * softplus
    prefix, suffix = g_all, g_all
    for s in (1, 2, 4, 8, 16, 32):
        prefix = prefix + jnp.where(cpos >= s, pltpu.roll(prefix, s, 0), 0.0)
        suffix = suffix + jnp.where(cpos < CHUNK - s, pltpu.roll(suffix, L - s, 0), 0.0)
    gc_s[...] = jnp.where(glane < 3 * NH, prefix, suffix)

    ii = lax.broadcasted_iota(jnp.int32, (CHUNK, 2 * CHUNK), 0)
    jj = lax.broadcasted_iota(jnp.int32, (CHUNK, 2 * CHUNK), 1) & (CHUNK - 1)
    clane = lax.broadcasted_iota(jnp.int32, (CHUNK, LANE), 1)
    diag = (ii >> 1) == (jj >> 1)
    off_masks = [((ii >> bits) ^ (jj >> bits)) == 1 for bits in range(1, 6)]

    def column(arr, c):
        picked = jnp.sum(jnp.where(clane == c, arr, 0.0), axis=1, keepdims=True)
        return jnp.broadcast_to(picked, (CHUNK, HD))

    chains = [(h, d) for h in range(hg) for d in range(2)]

    chunks_per_iter = DN_CHAINS // (2 * hg)

    def pre_body(m, carry):
        group = []
        for c in range(chunks_per_iter):
            n = m * chunks_per_iter + c
            r0 = pl.multiple_of(n * CHUNK, CHUNK)
            gcr = gc_s[pl.ds(r0, CHUNK), :]
            btr = bt_s[pl.ds(r0, CHUNK), :]
            for h in range(hg):
                qc = a_s[pl.ds(r0, CHUNK), h * HD:(h + 1) * HD]
                kc = b_s[pl.ds(r0, CHUNK), h * HD:(h + 1) * HD]
                raw = _mm_nt(jnp.concatenate([kc, qc], axis=0),
                             jnp.concatenate([kc, kc], axis=0))
                group += [(h, d, n, r0, gcr, btr, qc, kc, raw) for d in range(2)]
        pre_group(group)
        return carry

    def pre_group(group):
        a_l, x_l = [], []
        for h, d, n, r0, gcr, btr, qc, kc, raw in group:
            incl = (ii >= jj) if d == 0 else (ii <= jj)
            strict = (ii > jj) if d == 0 else (ii < jj)
            col = d * NH + head0 + h
            bb = column(btr, col)
            gcb = column(gcr, 2 * NH + col)
            gct = jnp.transpose(jnp.concatenate([gcb, gcb], axis=0))[:CHUNK, :]
            decay = jnp.where(incl, jnp.exp(jnp.where(incl, gcb - gct, 0.0)), 0.0)
            eg = jnp.exp(gcb)
            gend = gcb[CHUNK - 1:CHUNK, :] if d == 0 else gcb[0:1, :]
            vc = c_s[pl.ds(r0, CHUNK), h * HD:(h + 1) * HD]
            wq_s[d, h, n, CHUNK:, :] = (qc * eg).astype(BF)
            qkkd_s[d, h, n, 0:CHUNK, :] = (raw[CHUNK:, :CHUNK] * decay[:, :CHUNK]).astype(BF)
            qkkd_s[d, h, n, CHUNK:, :] = jnp.transpose(kc * jnp.exp(gend - gcb)).astype(BF)
            ge_s[d, h, n] = jnp.broadcast_to(jnp.exp(gend), (8, HD))
            a_l.append(jnp.where(strict, bb * raw[:CHUNK] * decay, 0.0))
            x_l.append(jnp.concatenate([vc * bb, kc * (bb * eg)], axis=1))
        def each(fn, *lists):
            return [fn(*vals) for vals in zip(*lists)]

        mm1 = lambda a, b: jnp.dot(a.astype(BF)[:, :CHUNK], b.astype(BF), preferred_element_type=F32)
        y_l = [jnp.where(ii == jj, 1.0, 0.0) - jnp.where(diag, a, 0.0) for a in a_l]
        for off_diag in off_masks:
            z_l = each(lambda t, a: mm1(t, jnp.where(off_diag, a, 0.0)), y_l, a_l)
            y_l = each(lambda t, z: t - mm1(z, t), y_l, z_l)

        def apply3(t, x):
            th, tl = _split(t)
            xh, xl = _split(x)
            return jnp.dot(jnp.concatenate([th, tl[:, :CHUNK]], axis=1),
                           jnp.concatenate([xh, xl, xh], axis=0), preferred_element_type=F32)
        x_l = each(apply3, y_l, x_l)
        for (h, d, n, r0, *_), x in zip(group, x_l):
            u_s[d, h, pl.ds(r0, CHUNK), :] = x[:, :HD]
            wq_s[d, h, n, 0:CHUNK, :] = x[:, HD:].astype(BF)

    lax.fori_loop(0, n_chunks // chunks_per_iter, pre_body, 0, unroll=DN_UNROLL)

    for h, d in chains:
        s_s[d, h] = s0_ref[d, h] if has_s0 else jnp.zeros((HD, HD), F32)

    def scan_body(n, carry):
        cs = [n, n_chunks - 1 - n]
        r0s = [pl.multiple_of(c * CHUNK, CHUNK) for c in cs]
        s_l = [s_s[d, h] for h, d in chains]
        ws_l = [jnp.dot(wq_s[d, h, cs[d]], s.astype(BF), preferred_element_type=F32)
                for (h, d), s in zip(chains, s_l)]
        v_l = [u_s[d, h, pl.ds(r0s[d], CHUNK), :] - ws[:CHUNK] for (h, d), ws in zip(chains, ws_l)]
        kv_l = [jnp.dot(qkkd_s[d, h, cs[d]], v.astype(BF), preferred_element_type=F32)
                for (h, d), v in zip(chains, v_l)]
        for (h, d), s, ws, kv in zip(chains, s_l, ws_l, kv_l):
            s_s[d, h] = s * ge_s[d, h, cs[d]][0:1] + kv[CHUNK:]
            out_s = a_s if d == 0 else b_s
            out_s[pl.ds(r0s[d], CHUNK), h * HD:(h + 1) * HD] = ws[CHUNK:] + kv[:CHUNK]
        return carry

    lax.fori_loop(0, n_chunks, scan_body, 0, unroll=4)

    if emit_state:
        for d in range(2):
            for h in range(hg):
                sfin_ref[d, h] = s_s[d, h]

    gdn = gdn_ref[...]
    for h in range(hg):
        hs = slice(h * HD, (h + 1) * HD)
        dg = dg_ref[:, hs]
        o_ref[:, hs] = (_rms(a_s[:, hs] + b_s[:, hs], gdn) * (dg * jax.nn.sigmoid(dg))).astype(BF)


def _dn_call(proj, w_conv, alog_l, dtb_l, gdn, state, l, *, n_batch, L, row_blk0, emit_state, prev_states=None):
    has_s0 = state is not None
    has_prev = prev_states is not None
    hg = DN_HEADS
    n_chunks = L // CHUNK
    W = hg * HD

    def pspec(cb):
        return pl.BlockSpec((L, W), lambda b, j: (row_blk0 + b, cb // hg + j))

    def wspec(off):
        return pl.BlockSpec((None, 3, W), lambda b, j: (l, 0, off // hg + j))

    vec = pl.BlockSpec((1, HD), lambda b, j: (0, 0))
    in_specs = [pspec(CB_DQ), pspec(CB_DK), pspec(CB_DV), pspec(CB_DG),
                pl.BlockSpec((L, LANE), lambda b, j: (row_blk0 + b, CB_GATE)),
                wspec(0), wspec(NH), wspec(2 * NH), vec, vec, vec]
    args = [proj, proj, proj, proj, proj, w_conv, w_conv, w_conv, alog_l, dtb_l, gdn]
    if has_s0:
        in_specs.append(pl.BlockSpec((None, None, 2, hg, HD, HD), lambda b, j: (b, l, 0, j, 0, 0)))
        args.append(state)
    aliases = {}
    if has_prev:
        aliases = {len(args): 1}
        in_specs.append(pl.BlockSpec(memory_space=pl.ANY))
        args.append(prev_states)
    out_specs = [pl.BlockSpec((L, W), lambda b, j: (b, j))]
    out_shape = [jax.ShapeDtypeStruct((n_batch * L, D), BF)]
    if emit_state:
        out_specs.append(pl.BlockSpec((None, None, 2, hg, HD, HD), lambda b, j: (b, l, 0, j, 0, 0)))
        out_shape.append(jax.ShapeDtypeStruct((n_batch, DEPTH, 2, NH, HD, HD), F32))
    return pl.pallas_call(
        functools.partial(_dn_kernel, L=L, hg=hg, has_s0=has_s0, has_prev=has_prev, emit_state=emit_state),
        grid=(n_batch, NH // hg),
        in_specs=in_specs,
        out_specs=out_specs,
        out_shape=out_shape,
        input_output_aliases=aliases,
        scratch_shapes=[
            pltpu.VMEM((L, W), F32), pltpu.VMEM((L, W), F32), pltpu.VMEM((L, W), F32),
            pltpu.VMEM((L, LANE), F32), pltpu.VMEM((L, LANE), F32),
            pltpu.VMEM((2, hg, L, HD), F32),
            pltpu.VMEM((2, hg, n_chunks, 2 * CHUNK, HD), BF),
            pltpu.VMEM((2, hg, n_chunks, CHUNK + HD, CHUNK), BF),
            pltpu.VMEM((2, hg, n_chunks, 8, HD), F32),
            pltpu.VMEM((2, hg, HD, HD), F32),
        ],
        compiler_params=_params("parallel", "parallel"),
        name="deltanet_ctx" if not has_s0 else "deltanet_smp",
    )(*args)


def _sc_kernel(sb_ref, scg_ref, sx_ref, w_ref, o_ref, *, L):
    tn = o_ref.shape[1]
    row = lax.broadcasted_iota(jnp.int32, (L, tn), 0)
    x = scg_ref[...] * sx_ref[...]
    w = w_ref[...]
    xm = jnp.where(row == 0, 0.0, pltpu.roll(x, 1, 0))
    xp = jnp.where(row == L - 1, 0.0, pltpu.roll(x, L - 1, 0))
    y = xm * w[0:1] + x * w[1:2] + xp * w[2:3]
    o_ref[...] = (sb_ref[...] * y).astype(BF)


SC_BLOCK_ELEMS = 256 * 1024


def _sc_call(proj, w_conv_sc, l, *, n_batch, L, row_blk0):
    tn = SC_BLOCK_ELEMS // L

    def pspec(cb):
        return pl.BlockSpec((L, tn), lambda b, j: (row_blk0 + b, cb * LANE // tn + j))

    return pl.pallas_call(
        functools.partial(_sc_kernel, L=L),
        grid=(n_batch, D // tn),
        in_specs=[pspec(CB_SB), pspec(CB_SCG), pspec(CB_SX),
                  pl.BlockSpec((None, 3, tn), lambda b, j: (l, 0, j))],
        out_specs=pl.BlockSpec((L, tn), lambda b, j: (b, j)),
        out_shape=jax.ShapeDtypeStruct((n_batch * L, D), BF),
        compiler_params=_params("parallel", "parallel"),
        name="shortconv",
    )(proj, proj, proj, w_conv_sc)


def _rope(x, cos, sin_signed):
    lane = lax.broadcasted_iota(jnp.int32, x.shape, 1)
    swapped = jnp.where((lane & 63) < 32, pltpu.roll(x, HD - 32, 1), pltpu.roll(x, 32, 1))
    return x * cos + swapped * sin_signed


def _softmax_pv(s_parts, v_parts):
    m = s_parts[0].max(axis=-1, keepdims=True)
    for s in s_parts[1:]:
        m = jnp.maximum(m, s.max(axis=-1, keepdims=True))
    den = None
    acc = None
    for s, v in zip(s_parts, v_parts):
        p = jnp.exp(s - m)
        ps = jnp.sum(p, axis=-1, keepdims=True)
        pv = _mm(p, v)
        den = ps if den is None else den + ps
        acc = pv if acc is None else acc + pv
    return acc / den


def _pipelined(items, scores, finish):
    s_next = scores(items[0])
    for n, item in enumerate(items):
        s_cur = s_next
        if n + 1 < len(items):
            s_next = scores(items[n + 1])
        finish(item, s_cur)


def _att_ctx_kernel(q_ref, k_ref, v_ref, gq_ref, gk_ref, *rest):
    o_ref, ko_ref, vo_ref = rest[-3:]
    gk = gk_ref[...]
    gq = gq_ref[...] * (HD ** -0.5)
    kbs, vbs = [], []
    for g in range(KVH):
        gs = slice(g * HD, (g + 1) * HD)
        k = _rms(k_ref[:, gs], gk)
        v = v_ref[:, gs]
        ko_ref[:, gs] = k
        vo_ref[:, gs] = v
        kbs.append(k.astype(BF))
        vbs.append(v.astype(BF))

    def scores(h):
        return [_mm_nt(_rms(q_ref[:, h * HD:(h + 1) * HD], gq), kbs[h // GQ])]

    def finish(h, s):
        o_ref[:, h * HD:(h + 1) * HD] = _softmax_pv(s, [vbs[h // GQ]]).astype(BF)

    _pipelined(list(range(NH)), scores, finish)


def _att_ctx_call(proj, g_q, g_k, l, caches):
    vec = pl.BlockSpec((1, HD), lambda b: (0, 0))
    kv_w = KVH * HD
    cache_spec = pl.BlockSpec((None, None, SEQ, kv_w), lambda b: (b, l, 0, 0))
    cache_shape = jax.ShapeDtypeStruct((BATCH, DEPTH, SEQ, kv_w), F32)
    in_specs = [
        pl.BlockSpec((SEQ, D), lambda b: (b, CB_AQ // NH)),
        pl.BlockSpec((SEQ, kv_w), lambda b: (b, CB_AK // KVH)),
        pl.BlockSpec((SEQ, kv_w), lambda b: (b, CB_AV // KVH)),
        vec, vec,
    ]
    args = [proj, proj, proj, g_q, g_k]
    aliases = {}
    if caches is not None:
        aliases = {len(args): 1, len(args) + 1: 2}
        in_specs += [pl.BlockSpec(memory_space=pl.ANY)] * 2
        args += list(caches)
    return pl.pallas_call(
        _att_ctx_kernel,
        grid=(BATCH,),
        in_specs=in_specs,
        out_specs=[pl.BlockSpec((SEQ, D), lambda b: (b, 0)), cache_spec, cache_spec],
        out_shape=[jax.ShapeDtypeStruct((T_CTX, D), BF), cache_shape, cache_shape],
        input_output_aliases=aliases,
        compiler_params=_params("parallel"),
        name="attention_ctx",
    )(*args)


QB = 256


def _att_smp_kernel(q_ref, k_ref, v_ref, ck_ref, cv_ref, gq_ref, gk_ref, cos_ref, sin_ref, o_ref):
    kb = _rope(_rms(k_ref[...], gk_ref[...]), cos_ref[...], sin_ref[...]).astype(BF)
    ckb = ck_ref[...].astype(BF)
    vs = [cv_ref[...].astype(BF), v_ref[...].astype(BF)]
    gq = gq_ref[...] * (HD ** -0.5)

    def scores(item):
        i, g = item
        rows = slice(i * QB, (i + 1) * QB)
        q = _rope(_rms(q_ref[rows, g * HD:(g + 1) * HD], gq), cos_ref[rows, :], sin_ref[rows, :]).astype(BF)
        return [_mm_nt(q, ckb), _mm_nt(q, kb)]

    def finish(item, s):
        i, g = item
        o_ref[i * QB:(i + 1) * QB, g * HD:(g + 1) * HD] = _softmax_pv(s, vs).astype(BF)

    _pipelined([(i, g) for i in range(DEC_SEQ // QB) for g in range(GQ)], scores, finish)


def _att_smp_call(proj, cache_k4, cache_v4, g_q, g_k, cos_t, sin_t, l):
    smp_blk0 = T_CTX // DEC_SEQ
    vec = pl.BlockSpec((1, HD), lambda b, g: (0, 0))
    table = pl.BlockSpec((DEC_SEQ, HD), lambda b, g: (0, 0))
    cache = pl.BlockSpec((None, None, PAST, HD), lambda b, g: (b, l, 0, g))
    return pl.pallas_call(
        _att_smp_kernel,
        grid=(DEC_BATCH, KVH),
        in_specs=[
            pl.BlockSpec((DEC_SEQ, GQ * HD), lambda b, g: (smp_blk0 + b, CB_AQ // GQ + g)),
            pl.BlockSpec((DEC_SEQ, HD), lambda b, g: (smp_blk0 + b, CB_AK + g)),
            pl.BlockSpec((DEC_SEQ, HD), lambda b, g: (smp_blk0 + b, CB_AV + g)),
            cache, cache, vec, vec, table, table,
        ],
        out_specs=pl.BlockSpec((DEC_SEQ, GQ * HD), lambda b, g: (b, g)),
        out_shape=jax.ShapeDtypeStruct((T_SMP, D), BF),
        compiler_params=_params("parallel", "parallel"),
        name="attention_smp",
    )(proj, proj, proj, cache_k4, cache_v4, g_q, g_k, cos_t, sin_t)


def _pick(i, ctx_ref, smp_ref):
    return jnp.where(i < N_CTX_TILES, ctx_ref[...], smp_ref[...])


def _mix_kernel(x_ref, ga_ref, gb_ref, gc_ref, ac_ref, as_ref, bc_ref, bs_ref, cc_ref, cs_ref,
                wa_ref, wb_ref, wc_ref, wo_ref, gpost_ref, gt_ref, o_ref):
    i = pl.program_id(0)
    m = jax.nn.sigmoid(ga_ref[...]) * jnp.dot(_pick(i, ac_ref, as_ref), wa_ref[...], preferred_element_type=F32)
    m += jax.nn.sigmoid(gb_ref[...]) * jnp.dot(_pick(i, bc_ref, bs_ref), wb_ref[...], preferred_element_type=F32)
    m += jax.nn.sigmoid(gc_ref[...]) * jnp.dot(_pick(i, cc_ref, cs_ref), wc_ref[...], preferred_element_type=F32)
    mix = jnp.dot(m.astype(BF), wo_ref[...], preferred_element_type=F32)
    o_ref[...] = x_ref[...] + gt_ref[...] * _rms(mix, gpost_ref[...])


def _resident(shape, index_map):
    return pl.BlockSpec(shape, index_map, pipeline_mode=pl.Buffered(1))


def _mix_call(x, proj, branches, weights, gain3, mod4, l):
    ctx_spec = pl.BlockSpec((TM, D), lambda i: (jnp.minimum(i, N_CTX_TILES - 1), 0))
    smp_spec = pl.BlockSpec((TM, D), lambda i: (jnp.maximum(i - N_CTX_TILES, 0), 0))
    wspec = _resident((D, D), lambda i: (0, 0))

    def gate_spec(k):
        return pl.BlockSpec((TM, D), lambda i: (i, CB_MG // NH + k))

    in_specs = [pl.BlockSpec((TM, D), lambda i: (i, 0)), gate_spec(0), gate_spec(1), gate_spec(2)]
    args = [x, proj, proj, proj]
    for ctx, smp in branches:
        in_specs += [ctx_spec, smp_spec]
        args += [ctx, smp]
    in_specs += [wspec] * 4 + [_gain_spec(l), _mod_spec(l, 2)]
    args += list(weights) + [gain3, mod4]
    return pl.pallas_call(
        _mix_kernel,
        grid=(N_TILES,),
        in_specs=in_specs,
        out_specs=pl.BlockSpec((TM, D), lambda i: (i, 0)),
        out_shape=jax.ShapeDtypeStruct((T_ALL, D), F32),
        compiler_params=_params("parallel"),
        name="merge_outproj",
    )(*args)


def _mlp_kernel(x_ref, gpre_ref, sh_ref, sc_ref, wu_ref, wd_ref, gpost_ref, gt_ref, *rest, last):
    x = x_ref[...]
    h = _rms(x, gpre_ref[...]) * (1.0 + sc_ref[...]) + sh_ref[...]
    u = jnp.maximum(jnp.dot(h.astype(BF), wu_ref[...], preferred_element_type=F32), 0.0)
    dn = jnp.dot((u * u).astype(BF), wd_ref[...], preferred_element_type=F32)
    y = x + gt_ref[...] * _rms(dn, gpost_ref[...])
    if last:
        yc_ref, ys_ref = rest
        i = pl.program_id(0)

        @pl.when(i < N_CTX_TILES)
        def _():
            yc_ref[...] = y

        @pl.when(i >= N_CTX_TILES)
        def _():
            ys_ref[...] = y
    else:
        gn_ref, shn_ref, scn_ref, y_ref, hn_ref = rest
        y_ref[...] = y
        hn_ref[...] = (_rms(y, gn_ref[...]) * (1.0 + scn_ref[...]) + shn_ref[...]).astype(BF)


def _mlp_call(x, w_up, w_down, gpre3, gpost3, gpre_mix3, mod4, l):
    last = l == DEPTH - 1
    row_spec = pl.BlockSpec((TM, D), lambda i: (i, 0))
    in_specs = [
        row_spec,
        _gain_spec(l), _mod_spec(l, 3), _mod_spec(l, 4),
        _resident((D, MLP_H), lambda i: (0, 0)),
        _resident((MLP_H, D), lambda i: (0, 0)),
        _gain_spec(l), _mod_spec(l, 5),
    ]
    args = [x, gpre3, mod4, mod4, w_up, w_down, gpost3, mod4]
    if last:
        out_specs = [pl.BlockSpec((TM, D), lambda i: (jnp.minimum(i, N_CTX_TILES - 1), 0)),
                     pl.BlockSpec((TM, D), lambda i: (jnp.maximum(i - N_CTX_TILES, 0), 0))]
        out_shape = [jax.ShapeDtypeStruct((T_CTX, D), F32), jax.ShapeDtypeStruct((T_SMP, D), F32)]
    else:
        in_specs += [_gain_spec(l + 1), _mod_spec(l + 1, 0), _mod_spec(l + 1, 1)]
        args += [gpre_mix3, mod4, mod4]
        out_specs = [row_spec, row_spec]
        out_shape = [jax.ShapeDtypeStruct((T_ALL, D), F32), jax.ShapeDtypeStruct((T_ALL, D), BF)]
    return pl.pallas_call(
        functools.partial(_mlp_kernel, last=last),
        grid=(N_TILES,),
        in_specs=in_specs,
        out_specs=out_specs,
        out_shape=out_shape,
        compiler_params=_params("arbitrary"),
        name="mlp",
    )(*args)


def _rope_tables():
    t = jnp.arange(DEC_SEQ)
    n_freq = HD // 4
    freqs = ROPE_THETA ** (-jnp.arange(n_freq, dtype=F32) / n_freq)
    ang_r = (t // GRID_W).astype(F32)[:, None] * freqs
    ang_c = (t % GRID_W).astype(F32)[:, None] * freqs
    cos_t = jnp.concatenate([jnp.cos(ang_r)] * 2 + [jnp.cos(ang_c)] * 2, axis=1)
    sin_t = jnp.concatenate([-jnp.sin(ang_r), jnp.sin(ang_r), -jnp.sin(ang_c), jnp.sin(ang_c)], axis=1)
    return cos_t, sin_t


def _gate_lanes(p):
    flat = p.reshape(DEPTH, 1, 2 * NH)
    return jnp.pad(flat, ((0, 0), (0, 0), (2 * NH, LANE - 4 * NH)))


def kernel(x_prompt, x_sample, cache_k, cache_v, state_dn, c, c_ctx, w_mod, b_mod, g_pre_mix, g_post_mix, g_pre_mlp, g_post_mlp, w_in, w_conv_dn, a_log, dt_bias, g_dn_out, w_conv_sc, g_q, g_k, w_br_dn, w_br_sc, w_br_att, w_out, w_mlp_up, w_mlp_down):
    x = jnp.concatenate([x_prompt.reshape(T_CTX, D), x_sample.reshape(T_SMP, D)], axis=0)

    cond8 = jnp.zeros((8, D), F32).at[0].set(c_ctx).at[1:1 + DEC_BATCH].set(c)
    mod4 = _mod_call(cond8, w_mod, b_mod).reshape(DEPTH, 8, 1, 6 * D)


    gpm3, gqm3 = g_pre_mix.reshape(DEPTH, 1, D), g_post_mix.reshape(DEPTH, 1, D)
    gpl3, gql3 = g_pre_mlp.reshape(DEPTH, 1, D), g_post_mlp.reshape(DEPTH, 1, D)
    alog_p, dtb_p = _gate_lanes(a_log), _gate_lanes(dt_bias)
    cos_t, sin_t = _rope_tables()
    cache_k4 = cache_k.reshape(DEC_BATCH, DEPTH, PAST, KVH * HD)
    cache_v4 = cache_v.reshape(DEC_BATCH, DEPTH, PAST, KVH * HD)

    w_t = jnp.swapaxes(w_in, 1, 2)

    caches, new_s = None, None
    h = _pre_call(x, gpm3, mod4, 0, 0, 1)
    for l in range(DEPTH):
        proj, w_up, w_down, *w_square = _proj_call(h, w_t, w_mlp_up, w_mlp_down,
                                                   (w_br_dn, w_br_sc, w_br_att, w_out), l)

        gdn = g_dn_out[l].reshape(1, HD)
        dn_ctx, new_s = _dn_call(proj, w_conv_dn, alog_p[l], dtb_p[l], gdn, None, l,
                                 n_batch=BATCH, L=SEQ, row_blk0=0, emit_state=True, prev_states=new_s)
        (dn_smp,) = _dn_call(proj, w_conv_dn, alog_p[l], dtb_p[l], gdn, state_dn, l,
                             n_batch=DEC_BATCH, L=DEC_SEQ, row_blk0=T_CTX // DEC_SEQ, emit_state=False)
        sc_ctx = _sc_call(proj, w_conv_sc, l, n_batch=BATCH, L=SEQ, row_blk0=0)
        sc_smp = _sc_call(proj, w_conv_sc, l, n_batch=DEC_BATCH, L=DEC_SEQ, row_blk0=T_CTX // DEC_SEQ)
        gq, gk = g_q[l].reshape(1, HD), g_k[l].reshape(1, HD)
        at_ctx, *caches = _att_ctx_call(proj, gq, gk, l, caches)
        at_smp = _att_smp_call(proj, cache_k4, cache_v4, gq, gk, cos_t, sin_t, l)

        x = _mix_call(x, proj, [(dn_ctx, dn_smp), (sc_ctx, sc_smp), (at_ctx, at_smp)],
                      w_square, gqm3, mod4, l)
        x, h = _mlp_call(x, w_up, w_down, gpl3, gql3, gpm3, mod4, l)

    y_ctx, y_smp = x, h
    new_k, new_v = (t.reshape(BATCH, DEPTH, SEQ, KVH, HD) for t in caches)
    return (y_ctx.reshape(BATCH, SEQ, D), y_smp.reshape(DEC_BATCH, DEC_SEQ, D), new_k, new_v, new_s)
```
